```python
import math
import jax, jax.numpy as jnp
from jax import lax
import numpy as np

D_MODEL = 1024
BATCH = 8
SEQ = 2048
DEPTH = 2
DEC_BATCH = 128
DEC_SEQ = 1
PAST_LEN = 16384
PAGE_SIZE = 128

SSM_INNER = 2 * D_MODEL
HEAD_DIM = 64
SSM_HEADS = SSM_INNER // HEAD_DIM
SSM_GROUPS = 4
D_STATE = 128
SSM_CONV = 4
CONV_DIM = SSM_INNER + 2 * SSM_GROUPS * D_STATE
SSM_PROJ = 2 * SSM_INNER + 2 * SSM_GROUPS * D_STATE + SSM_HEADS
CHUNK = 128
CFM_WIDTH = 2 * D_MODEL
CFM_KERNEL = 31
N_SSM_LAYERS = (DEPTH + 1) // 2
N_CFM_LAYERS = DEPTH // 2
EPS = 1e-6

kernel_name = "hybrid_ssd_conformer_decode_step"


def rms_norm(x, g):
    xf = x.astype(jnp.float32)
    y = xf * lax.rsqrt(jnp.mean(xf * xf, axis=-1, keepdims=True) + EPS)
    return (y * g.astype(jnp.float32)).astype(x.dtype)


def layer_norm(x, g, b):
    xf = x.astype(jnp.float32)
    mu = jnp.mean(xf, axis=-1, keepdims=True)
    xc = xf - mu
    y = xc * lax.rsqrt(jnp.mean(xc * xc, axis=-1, keepdims=True) + EPS)
    return (y * g.astype(jnp.float32) + b.astype(jnp.float32)).astype(x.dtype)


def causal_depthwise_conv(x, buf, w, b):
    k = w.shape[0]
    xp = jnp.concatenate([buf.astype(x.dtype), x], axis=1)
    out = lax.conv_general_dilated(
        xp, w[:, None, :].astype(x.dtype), window_strides=(1,), padding='VALID',
        dimension_numbers=('NWC', 'WIO', 'NWC'), feature_group_count=x.shape[-1])
    return out + b.astype(x.dtype), xp[:, xp.shape[1] - (k - 1):, :]


def ssd_scan(x, dt, A, Bm, Cm, h0, chunk):
    b, L, H, P = x.shape
    G, N = Bm.shape[-2], Bm.shape[-1]
    R = H // G
    nc = L // chunk
    f32 = jnp.float32
    xs = x.astype(f32).reshape(b, nc, chunk, G, R, P)
    dts = dt.astype(f32).reshape(b, nc, chunk, G, R)
    Bs = Bm.astype(f32).reshape(b, nc, chunk, G, N)
    Cs = Cm.astype(f32).reshape(b, nc, chunk, G, N)
    a_cum = jnp.cumsum(dts * A.reshape(G, R), axis=2)
    xdt = xs * dts[..., None]
    diff = a_cum[:, :, :, None] - a_cum[:, :, None, :]
    mask = jnp.tril(jnp.ones((chunk, chunk), dtype=bool))[:, :, None, None]
    decay = jnp.exp(jnp.where(mask, diff, -jnp.inf))
    cb = jnp.einsum('bcqgn,bckgn->bcqkg', Cs, Bs)
    y_diag = jnp.einsum('bcqkg,bcqkgr,bckgrp->bcqgrp', cb, decay, xdt)
    states = jnp.einsum('bcqgn,bcqgr,bcqgrp->bcgrpn', Bs, jnp.exp(a_cum[:, :, -1:] - a_cum), xdt)
    chunk_decay = jnp.exp(a_cum[:, :, -1])

    def step(h, inp):
        dec, st = inp
        return dec[..., None, None] * h + st, h

    h_init = h0.astype(f32).reshape(b, G, R, P, N)
    h_fin, prev = lax.scan(step, h_init, (jnp.moveaxis(chunk_decay, 1, 0), jnp.moveaxis(states, 1, 0)))
    prev = jnp.moveaxis(prev, 0, 1)
    y_off = jnp.einsum('bcqgn,bcgrpn,bcqgr->bcqgrp', Cs, prev, jnp.exp(a_cum))
    y = (y_diag + y_off).reshape(b, L, H, P)
    return y, h_fin.reshape(b, H, P, N)


def mamba2_mixer(u, conv_buf, h0, chunk, w_in, conv_w, conv_b, dt_bias, A_log, D_skip, norm_g, w_out):
    b, L, _ = u.shape
    zxbcdt = u @ w_in
    z, xbc, dt = jnp.split(zxbcdt, [SSM_INNER, SSM_INNER + CONV_DIM], axis=-1)
    xbc, new_buf = causal_depthwise_conv(xbc, conv_buf, conv_w, conv_b)
    xbc = jax.nn.silu(xbc)
    xs, Bm, Cm = jnp.split(xbc, [SSM_INNER, SSM_INNER + SSM_GROUPS * D_STATE], axis=-1)
    xs = xs.reshape(b, L, SSM_HEADS, HEAD_DIM)
    Bm = Bm.reshape(b, L, SSM_GROUPS, D_STATE)
    Cm = Cm.reshape(b, L, SSM_GROUPS, D_STATE)
    dt = jax.nn.softplus(dt.astype(jnp.float32) + dt_bias.astype(jnp.float32))
    A = -jnp.exp(A_log.astype(jnp.float32))
    y, h_new = ssd_scan(xs, dt, A, Bm, Cm, h0, chunk)
    y = y + D_skip.astype(jnp.float32)[:, None] * xs.astype(jnp.float32)
    y = y.reshape(b, L, SSM_INNER).astype(u.dtype)
    y = rms_norm(y * jax.nn.silu(z), norm_g)
    return y @ w_out, new_buf, h_new


def conformer_conv_mixer(u, conv_buf, w_in, b_in, conv_w, conv_b, ln_g, ln_b, w_out):
    proj = u @ w_in + b_in
    val, glu_gate, z = jnp.split(proj, 3, axis=-1)
    v = val * jax.nn.sigmoid(glu_gate)
    c, new_buf = causal_depthwise_conv(v, conv_buf, conv_w, conv_b)
    c = jax.nn.silu(layer_norm(c, ln_g, ln_b))
    return (c * jax.nn.silu(z)) @ w_out, new_buf


def setup_inputs(seed: int = 0) -> dict:
    key = jax.random.key(seed)
    ks = jax.random.split(key, 24)
    f32 = jnp.float32
    nrm = lambda k, s, sc: jax.random.normal(k, s, f32) * sc
    dt0 = jnp.exp(jax.random.uniform(ks[10], (N_SSM_LAYERS, SSM_HEADS), f32, math.log(1e-3), math.log(1e-1)))
    return {
        "x_prompt": nrm(ks[0], (BATCH, SEQ, D_MODEL), 1.0),
        "x_sample": nrm(ks[1], (DEC_BATCH, DEC_SEQ, D_MODEL), 1.0),
        "state_ssm": nrm(ks[2], (N_SSM_LAYERS, DEC_BATCH, SSM_HEADS, HEAD_DIM, D_STATE), 0.5),
        "state_conv_ssm": nrm(ks[3], (N_SSM_LAYERS, DEC_BATCH, SSM_CONV - 1, CONV_DIM), 1.0),
        "state_conv_cfm": nrm(ks[4], (N_CFM_LAYERS, DEC_BATCH, CFM_KERNEL - 1, CFM_WIDTH), 0.5),
        "g_pre": 1.0 + nrm(ks[5], (DEPTH, D_MODEL), 0.01),
        "g_post": 1.0 + nrm(ks[6], (DEPTH, D_MODEL), 0.01),
        "ssm_w_in": nrm(ks[7], (N_SSM_LAYERS, D_MODEL, SSM_PROJ), D_MODEL ** -0.5),
        "ssm_conv_w": nrm(ks[8], (N_SSM_LAYERS, SSM_CONV, CONV_DIM), SSM_CONV ** -0.5),
        "ssm_conv_b": nrm(ks[9], (N_SSM_LAYERS, CONV_DIM), 0.02),
        "ssm_dt_bias": dt0 + jnp.log(-jnp.expm1(-dt0)),
        "ssm_A_log": jnp.log(jax.random.uniform(ks[11], (N_SSM_LAYERS, SSM_HEADS), f32, 1.0, 16.0)),
        "ssm_D": 1.0 + nrm(ks[12], (N_SSM_LAYERS, SSM_HEADS), 0.01),
        "ssm_norm_g": 1.0 + nrm(ks[13], (N_SSM_LAYERS, SSM_INNER), 0.01),
        "ssm_w_out": nrm(ks[14], (N_SSM_LAYERS, SSM_INNER, D_MODEL), SSM_INNER ** -0.5),
        "cfm_w_in": nrm(ks[15], (N_CFM_LAYERS, D_MODEL, 3 * CFM_WIDTH), D_MODEL ** -0.5),
        "cfm_b_in": nrm(ks[16], (N_CFM_LAYERS, 3 * CFM_WIDTH), 0.02),
        "cfm_conv_w": nrm(ks[17], (N_CFM_LAYERS, CFM_KERNEL, CFM_WIDTH), CFM_KERNEL ** -0.5),
        "cfm_conv_b": nrm(ks[18], (N_CFM_LAYERS, CFM_WIDTH), 0.02),
        "cfm_ln_g": 1.0 + nrm(ks[19], (N_CFM_LAYERS, CFM_WIDTH), 0.01),
        "cfm_ln_b": nrm(ks[20], (N_CFM_LAYERS, CFM_WIDTH), 0.01),
        "cfm_w_out": nrm(ks[21], (N_CFM_LAYERS, CFM_WIDTH, D_MODEL), CFM_WIDTH ** -0.5),
    }


def reference(x_prompt, x_sample, state_ssm, state_conv_ssm, state_conv_cfm, g_pre, g_post,
              ssm_w_in, ssm_conv_w, ssm_conv_b, ssm_dt_bias, ssm_A_log, ssm_D, ssm_norm_g, ssm_w_out,
              cfm_w_in, cfm_b_in, cfm_conv_w, cfm_conv_b, cfm_ln_g, cfm_ln_b, cfm_w_out):
    xp, xs = x_prompt, x_sample
    bp = x_prompt.shape[0]
    prompt_chunk = min(CHUNK, x_prompt.shape[1])
    sample_chunk = x_sample.shape[1]
    ssm_p, cssm_p, ccfm_p, ssm_s, cssm_s, ccfm_s = [], [], [], [], [], []
    for i in range(DEPTH):
        j = i // 2
        if i % 2 == 0:
            w = (ssm_w_in[j], ssm_conv_w[j], ssm_conv_b[j], ssm_dt_bias[j], ssm_A_log[j],
                 ssm_D[j], ssm_norm_g[j], ssm_w_out[j])
            buf0 = jnp.zeros((bp, SSM_CONV - 1, CONV_DIM), xp.dtype)
            h0 = jnp.zeros((bp, SSM_HEADS, HEAD_DIM, D_STATE), jnp.float32)
            op, nb_p, nh_p = mamba2_mixer(rms_norm(xp, g_pre[i]), buf0, h0, prompt_chunk, *w)
            os_, nb_s, nh_s = mamba2_mixer(rms_norm(xs, g_pre[i]), state_conv_ssm[j], state_ssm[j], sample_chunk, *w)
            ssm_p.append(nh_p); cssm_p.append(nb_p)
            ssm_s.append(nh_s.astype(state_ssm.dtype)); cssm_s.append(nb_s)
        else:
            w = (cfm_w_in[j], cfm_b_in[j], cfm_conv_w[j], cfm_conv_b[j], cfm_ln_g[j], cfm_ln_b[j], cfm_w_out[j])
            buf0 = jnp.zeros((bp, CFM_KERNEL - 1, CFM_WIDTH), xp.dtype)
            op, nb_p = conformer_conv_mixer(rms_norm(xp, g_pre[i]), buf0, *w)
            os_, nb_s = conformer_conv_mixer(rms_norm(xs, g_pre[i]), state_conv_cfm[j], *w)
            ccfm_p.append(nb_p); ccfm_s.append(nb_s)
        xp = xp + rms_norm(op, g_post[i])
        xs = xs + rms_norm(os_, g_post[i])
    new_ssm_prompt = jnp.stack(ssm_p)
    new_conv_ssm_prompt = jnp.stack(cssm_p)
    new_conv_cfm_prompt = jnp.stack(ccfm_p)
    new_ssm_sample = jnp.stack(ssm_s)
    new_conv_ssm_sample = jnp.stack(cssm_s)
    new_conv_cfm_sample = jnp.stack(ccfm_s)
    return (xp, xs, new_ssm_prompt, new_conv_ssm_prompt, new_conv_cfm_prompt,
            new_ssm_sample, new_conv_ssm_sample, new_conv_cfm_sample)
```

```python
import functools

import jax
import jax.numpy as jnp
from jax import lax
from jax.experimental import pallas as pl
from jax.experimental.pallas import tpu as pltpu

F32 = jnp.float32
BF16 = jnp.bfloat16

EPS = 1e-6
LANES = 128
CHUNK = 128
HEAD_DIM = 64
D_STATE = 128
N_GROUPS = 4
HEADS_PER_GROUP = 8
VMEM_LIMIT_BYTES = 56 * 1024 * 1024


def _dot(a, b):
    return jnp.dot(a, b, preferred_element_type=F32)


def _dot_nt(a, b):
    return lax.dot_general(a, b, (((1,), (1,)), ((), ())), preferred_element_type=F32)


def _dot_tn(a, b):
    return lax.dot_general(a, b, (((0,), (0,)), ((), ())), preferred_element_type=F32)


def _rms(x, g):
    return x * lax.rsqrt(jnp.mean(x * x, axis=-1, keepdims=True) + EPS) * g


def _layer_norm(x, g, b):
    mu = jnp.mean(x, axis=-1, keepdims=True)
    xc = x - mu
    return xc * lax.rsqrt(jnp.mean(xc * xc, axis=-1, keepdims=True) + EPS) * g + b


def _sigmoid(x):
    return 1.0 / (1.0 + jnp.exp(-x))


def _silu(x):
    return x * _sigmoid(x)


def _softplus(x):
    return jnp.maximum(x, 0.0) + jnp.log1p(jnp.exp(-jnp.abs(x)))


def _dot_f32_rhs(l_bf16, a):
    hi = a.astype(BF16)
    r1 = a - hi.astype(F32)
    mid = r1.astype(BF16)
    lo = (r1 - mid.astype(F32)).astype(BF16)
    return _dot(l_bf16, hi) + _dot(l_bf16, mid) + _dot(l_bf16, lo)


def _dot_f32_lhs(a, r_bf16):
    hi = a.astype(BF16)
    r1 = a - hi.astype(F32)
    mid = r1.astype(BF16)
    lo = (r1 - mid.astype(F32)).astype(BF16)
    return _dot(hi, r_bf16) + _dot(mid, r_bf16) + _dot(lo, r_bf16)


def _col(a, j):
    return a[:, j:j + 1]


def _ssd_prompt_kernel(x_ref, gpre_ref, wz_ref, wxbc_ref, wdt_ref, cw_ref, cb_ref,
                       dtb_ref, alog_ref, dexp_ref, ng_ref, wout_ref, gpost_ref,
                       y_ref, hout_ref, cout_ref,
                       ht_s, ext_s, xs_s, b_s, c_s, z_s, dt_s, a_s, yb_s):
    t = pl.program_id(1)
    nt = pl.num_programs(1)
    tile = x_ref.shape[0]
    n_chunks = tile // CHUNK
    d_inner = xs_s.shape[1]
    gn = N_GROUPS * D_STATE
    kw = cw_ref.shape[0]
    halo = 8

    @pl.when(t == 0)
    def _():
        ht_s[...] = jnp.zeros_like(ht_s)
        ext_s[0:halo, :] = jnp.zeros((halo, ext_s.shape[1]), F32)

    x = x_ref[...]
    u = _rms(x, gpre_ref[...]).astype(BF16)
    z_s[...] = _dot(u, wz_ref[...])
    xbc = _dot(u, wxbc_ref[...])
    dt_raw = _dot(u, wdt_ref[...])

    ext_s[halo:halo + tile, :] = xbc
    acc = jnp.broadcast_to(cb_ref[...], xbc.shape)
    for k in range(kw):
        acc = acc + cw_ref[k:k + 1, :] * ext_s[pl.ds(halo - (kw - 1) + k, tile), :]
    cout_ref[...] = ext_s[pl.ds(halo + tile - (kw - 1), kw - 1), :]
    ext_s[0:halo, :] = xbc[tile - halo:, :]
    act = _silu(acc)
    xs_s[...] = act[:, :d_inner]
    b_s[...] = act[:, d_inner:d_inner + gn].astype(BF16)
    c_s[...] = act[:, d_inner + gn:].astype(BF16)
    dt = _softplus(dt_raw + dtb_ref[...])
    dt_s[...] = dt
    a_s[...] = dt * (-jnp.exp(alog_ref[...]))

    row = lax.broadcasted_iota(jnp.int32, (CHUNK, CHUNK), 0)
    lane = lax.broadcasted_iota(jnp.int32, (CHUNK, CHUNK), 1)
    causal = row >= lane
    tri = causal.astype(BF16)
    lo_half = lane < HEAD_DIM
    lo_half_row = lo_half[0:1, :]
    dexp = dexp_ref[...]

    def chunk_body(c, carry):
        r0 = pl.multiple_of(c * CHUNK, CHUNK)
        rows = pl.ds(r0, CHUNK)
        a_c = a_s[rows, :]
        dt_c = dt_s[rows, :]
        a_cum = _dot_f32_rhs(tri, a_c)
        a_cum_t = a_cum.T
        dt_t = dt_c.T
        a_last = a_cum[CHUNK - 1:CHUNK, :]
        wq = jnp.exp(a_last - a_cum) * dt_c
        eq = jnp.exp(a_cum)
        edec = jnp.exp(a_last)
        for g in range(N_GROUPS):
            gl = slice(g * D_STATE, (g + 1) * D_STATE)
            cg = c_s[rows, gl]
            bg = b_s[rows, gl]
            cb = _dot_nt(cg, bg)
            ht_g = ht_s[g]
            yoff_g = _dot(cg, ht_g.astype(BF16))
            xw_parts = []
            dec_parts = []
            for j in range(HEADS_PER_GROUP // 2):
                h0 = g * HEADS_PER_GROUP + 2 * j
                h1 = h0 + 1
                pl_ = slice((g * 4 + j) * LANES, (g * 4 + j + 1) * LANES)
                xs_p = xs_s[rows, pl_]

                def scores(h):
                    diff = _col(a_cum, h) - a_cum_t[h:h + 1, :]
                    dec = jnp.exp(jnp.where(causal, diff, -jnp.inf))
                    return (cb * dec * dt_t[h:h + 1, :]).astype(BF16)

                x0 = jnp.where(lo_half, xs_p, 0.0).astype(BF16)
                x1 = jnp.where(lo_half, 0.0, xs_p).astype(BF16)
                y_diag = _dot(scores(h0), x0) + _dot(scores(h1), x1)
                e_pair = jnp.where(lo_half, _col(eq, h0), _col(eq, h1))
                y_pair = (y_diag + e_pair * yoff_g[:, j * LANES:(j + 1) * LANES]
                          + dexp[:, pl_] * xs_p)
                yb_s[rows, pl_] = y_pair
                w_pair = jnp.where(lo_half, _col(wq, h0), _col(wq, h1))
                xw_parts.append((xs_p * w_pair).astype(BF16))
                dec_parts.append(jnp.where(lo_half_row, edec[:, h0:h0 + 1], edec[:, h1:h1 + 1]))
            xw_g = jnp.concatenate(xw_parts, axis=1)
            dec_g = jnp.concatenate(dec_parts, axis=1)
            ht_s[g] = ht_g * dec_g + _dot_tn(bg, xw_g)
        return carry

    lax.fori_loop(0, n_chunks, chunk_body, 0)

    yg = yb_s[...] * _silu(z_s[...])
    yn = _rms(yg, ng_ref[...]).astype(BF16)
    o = _dot(yn, wout_ref[...])
    y_ref[...] = x + _rms(o, gpost_ref[...])

    @pl.when(t == nt - 1)
    def _():
        for g in range(N_GROUPS):
            hg = ht_s[g].T
            for r in range(HEADS_PER_GROUP):
                hout_ref[g * HEADS_PER_GROUP + r] = hg[r * HEAD_DIM:(r + 1) * HEAD_DIM, :]


def _const_spec(shape):
    nd = len(shape)
    return pl.BlockSpec(shape, lambda *_: (0,) * nd, pipeline_mode=pl.Buffered(1))


def _ssd_prompt_layer(x, gpre, wz, wxbc, wdt, cw, cb, dtb, alog, dexp, ng, wout, gpost, *, tile):
    bsz, seq, d = x.shape
    d_inner = wz.shape[1]
    conv_dim = wxbc.shape[1]
    gn = N_GROUPS * D_STATE
    heads = d_inner // HEAD_DIM
    kw = cw.shape[0]
    consts = (gpre, wz, wxbc, wdt, cw, cb, dtb, alog, dexp, ng, wout, gpost)
    return pl.pallas_call(
        _ssd_prompt_kernel,
        grid=(bsz, seq // tile),
        in_specs=[pl.BlockSpec((None, tile, d), lambda b, t: (b, t, 0))]
        + [_const_spec(c.shape) for c in consts],
        out_specs=[
            pl.BlockSpec((None, tile, d), lambda b, t: (b, t, 0)),
            pl.BlockSpec((None, heads, HEAD_DIM, D_STATE), lambda b, t: (b, 0, 0, 0)),
            pl.BlockSpec((None, kw - 1, conv_dim), lambda b, t: (b, 0, 0)),
        ],
        out_shape=[
            jax.ShapeDtypeStruct((bsz, seq, d), F32),
            jax.ShapeDtypeStruct((bsz, heads, HEAD_DIM, D_STATE), F32),
            jax.ShapeDtypeStruct((bsz, kw - 1, conv_dim), F32),
        ],
        scratch_shapes=[
            pltpu.VMEM((N_GROUPS, D_STATE, HEADS_PER_GROUP * HEAD_DIM), F32),
            pltpu.VMEM((8 + tile, conv_dim), F32),
            pltpu.VMEM((tile, d_inner), F32),
            pltpu.VMEM((tile, gn), BF16),
            pltpu.VMEM((tile, gn), BF16),
            pltpu.VMEM((tile, d_inner), F32),
            pltpu.VMEM((tile, LANES), F32),
            pltpu.VMEM((tile, LANES), F32),
            pltpu.VMEM((tile, d_inner), F32),
        ],
        compiler_params=pltpu.CompilerParams(
            dimension_semantics=("arbitrary", "arbitrary"),
            vmem_limit_bytes=VMEM_LIMIT_BYTES),
        name="ssd_prompt_layer",
    )(x, *consts)


def _cfm_prompt_kernel(x_ref, gpre_ref, wv_ref, wg_ref, wz_ref, bv_ref, bg_ref, bz_ref,
                       cw_ref, cb_ref, lng_ref, lnb_ref, wout_ref, gpost_ref,
                       y_ref, cout_ref, ext_s):
    t = pl.program_id(1)
    tile = x_ref.shape[0]
    kw = cw_ref.shape[0]
    halo = 32

    @pl.when(t == 0)
    def _():
        ext_s[0:halo, :] = jnp.zeros((halo, ext_s.shape[1]), F32)

    x = x_ref[...]
    u = _rms(x, gpre_ref[...]).astype(BF16)
    val = _dot(u, wv_ref[...]) + bv_ref[...]
    gate = _dot(u, wg_ref[...]) + bg_ref[...]
    z = _dot(u, wz_ref[...]) + bz_ref[...]
    v = val * _sigmoid(gate)

    ext_s[halo:halo + tile, :] = v
    acc = jnp.broadcast_to(cb_ref[...], v.shape)
    for k in range(kw):
        acc = acc + cw_ref[k:k + 1, :] * ext_s[pl.ds(halo - (kw - 1) + k, tile), :]
    cout_ref[...] = ext_s[pl.ds(halo + tile - (kw - 1), kw - 1), :]
    ext_s[0:halo, :] = v[tile - halo:, :]

    c = _silu(_layer_norm(acc, lng_ref[...], lnb_ref[...]))
    o = _dot((c * _silu(z)).astype(BF16), wout_ref[...])
    y_ref[...] = x + _rms(o, gpost_ref[...])


def _cfm_prompt_layer(x, gpre, wv, wg, wz, bv, bg, bz, cw, cb, lng, lnb, wout, gpost, *, tile):
    bsz, seq, d = x.shape
    width = wv.shape[1]
    kw = cw.shape[0]
    consts = (gpre, wv, wg, wz, bv, bg, bz, cw, cb, lng, lnb, wout, gpost)
    return pl.pallas_call(
        _cfm_prompt_kernel,
        grid=(bsz, seq // tile),
        in_specs=[pl.BlockSpec((None, tile, d), lambda b, t: (b, t, 0))]
        + [_const_spec(c.shape) for c in consts],
        out_specs=[
            pl.BlockSpec((None, tile, d), lambda b, t: (b, t, 0)),
            pl.BlockSpec((None, kw - 1, width), lambda b, t: (b, 0, 0)),
        ],
        out_shape=[
            jax.ShapeDtypeStruct((bsz, seq, d), F32),
            jax.ShapeDtypeStruct((bsz, kw - 1, width), F32),
        ],
        scratch_shapes=[pltpu.VMEM((32 + tile, width), F32)],
        compiler_params=pltpu.CompilerParams(
            dimension_semantics=("arbitrary", "arbitrary"),
            vmem_limit_bytes=VMEM_LIMIT_BYTES),
        name="cfm_prompt_layer",
    )(x, *consts)


def _single_step_call(kernel, out_shapes, *args, name):
    return pl.pallas_call(
        kernel,
        grid=(1,),
        in_specs=[_const_spec(a.shape) for a in args],
        out_specs=[pl.BlockSpec(s.shape, functools.partial(lambda nd, i: (0,) * nd, len(s.shape)))
                   for s in out_shapes],
        out_shape=out_shapes,
        compiler_params=pltpu.CompilerParams(
            dimension_semantics=("arbitrary",),
            vmem_limit_bytes=VMEM_LIMIT_BYTES),
        name=name,
    )(*args)


def _ssd_sample_front_kernel(x_ref, gpre_ref, wz_ref, wxbc_ref, wdt_ref, cst_ref, cw_ref, cb_ref,
                             dtb_ref, alog_ref,
                             z_ref, xs_ref, b_ref, c_ref, xdt_t_ref, da_t_ref, cst_out_ref):
    d_inner = xs_ref.shape[1]
    conv_dim = wxbc_ref.shape[1]
    gn = N_GROUPS * D_STATE
    kw = cw_ref.shape[0]
    u = _rms(x_ref[...], gpre_ref[...]).astype(BF16)
    z_ref[...] = _dot(u, wz_ref[...])
    xbc = _dot(u, wxbc_ref[...])
    dt_raw = _dot(u, wdt_ref[...])

    acc = cb_ref[...] + cw_ref[kw - 1:kw, :] * xbc
    for k in range(kw - 1):
        acc = acc + cw_ref[k:k + 1, :] * cst_ref[:, k * conv_dim:(k + 1) * conv_dim]
    cst_out_ref[:, :(kw - 2) * conv_dim] = cst_ref[:, conv_dim:]
    cst_out_ref[:, (kw - 2) * conv_dim:] = xbc
    act = _silu(acc)
    xs = act[:, :d_inner]
    xs_ref[...] = xs
    b_ref[...] = act[:, d_inner:d_inner + gn]
    c_ref[...] = act[:, d_inner + gn:]

    dt = _softplus(dt_raw + dtb_ref[...])
    a = dt * (-jnp.exp(alog_ref[...]))
    hrow = lax.broadcasted_iota(jnp.int32, (LANES, d_inner), 0)
    ccol = lax.broadcasted_iota(jnp.int32, (LANES, d_inner), 1)
    expand = ((ccol >> 6) == hrow).astype(BF16)
    dt_x = _dot_f32_lhs(dt, expand)
    a_x = _dot_f32_lhs(a, expand)
    xdt_t_ref[...] = (xs * dt_x).T
    da_t_ref[...] = jnp.exp(a_x).T


def _ssd_sample_state_kernel(h_ref, xdt_t_ref, da_t_ref, b_ref, c_ref, hn_ref, yt_ref):
    tok = pl.program_id(0)
    rows_per_group = HEADS_PER_GROUP * HEAD_DIM
    lane = lax.broadcasted_iota(jnp.int32, xdt_t_ref.shape, 1)
    sel = lane == tok
    xcol = jnp.sum(jnp.where(sel, xdt_t_ref[...], 0.0), axis=1, keepdims=True)
    acol = jnp.sum(jnp.where(sel, da_t_ref[...], 0.0), axis=1, keepdims=True)
    bb = b_ref[...]
    cc = c_ref[...]
    b_x = jnp.concatenate(
        [jnp.broadcast_to(bb[g:g + 1, :], (rows_per_group, D_STATE)) for g in range(N_GROUPS)], axis=0)
    c_x = jnp.concatenate(
        [jnp.broadcast_to(cc[g:g + 1, :], (rows_per_group, D_STATE)) for g in range(N_GROUPS)], axis=0)
    hn = acol * h_ref[...] + xcol * b_x
    hn_ref[...] = hn
    ycol = jnp.sum(hn * c_x, axis=1, keepdims=True)

    @pl.when(tok == 0)
    def _():
        yt_ref[...] = jnp.zeros_like(yt_ref)

    yt_ref[...] = jnp.where(sel, ycol, yt_ref[...])


def _ssd_sample_state(h0, xdt_t, da_t, b3, c3):
    ntok, rows, n = h0.shape
    return pl.pallas_call(
        _ssd_sample_state_kernel,
        grid=(ntok,),
        in_specs=[
            pl.BlockSpec((None, rows, n), lambda i: (i, 0, 0)),
            _const_spec(xdt_t.shape),
            _const_spec(da_t.shape),
            pl.BlockSpec((None,) + b3.shape[1:], lambda i: (i, 0, 0)),
            pl.BlockSpec((None,) + c3.shape[1:], lambda i: (i, 0, 0)),
        ],
        out_specs=[
            pl.BlockSpec((None, rows, n), lambda i: (i, 0, 0)),
            pl.BlockSpec(xdt_t.shape, lambda i: (0, 0)),
        ],
        out_shape=[
            jax.ShapeDtypeStruct(h0.shape, F32),
            jax.ShapeDtypeStruct(xdt_t.shape, F32),
        ],
        compiler_params=pltpu.CompilerParams(
            dimension_semantics=("arbitrary",),
            vmem_limit_bytes=VMEM_LIMIT_BYTES),
        name="ssd_sample_state",
    )(h0, xdt_t, da_t, b3, c3)


def _ssd_sample_tail_kernel(yt_ref, xs_ref, z_ref, x_ref, dexp_ref, ng_ref, wout_ref, gpost_ref, o_ref):
    y = yt_ref[...].T + dexp_ref[...] * xs_ref[...]
    yg = y * _silu(z_ref[...])
    yn = _rms(yg, ng_ref[...]).astype(BF16)
    o = _dot(yn, wout_ref[...])
    o_ref[...] = x_ref[...] + _rms(o, gpost_ref[...])


def _cfm_sample_front_kernel(x_ref, gpre_ref, wv_ref, wg_ref, wz_ref, bv_ref, bg_ref, bz_ref,
                             v_ref, z_ref):
    u = _rms(x_ref[...], gpre_ref[...]).astype(BF16)
    val = _dot(u, wv_ref[...]) + bv_ref[...]
    gate = _dot(u, wg_ref[...]) + bg_ref[...]
    z_ref[...] = _dot(u, wz_ref[...]) + bz_ref[...]
    v_ref[...] = val * _sigmoid(gate)


def _cfm_sample_conv_kernel(cst_ref, v_ref, cw_ref, cb_ref, c_ref, cst_out_ref):
    width = v_ref.shape[1]
    kw = cw_ref.shape[0]
    v = v_ref[...]
    acc = cb_ref[...] + cw_ref[kw - 1:kw, :] * v
    for k in range(kw - 1):
        acc = acc + cw_ref[k:k + 1, :] * cst_ref[:, k * width:(k + 1) * width]
    c_ref[...] = acc
    cst_out_ref[:, :(kw - 2) * width] = cst_ref[:, width:]
    cst_out_ref[:, (kw - 2) * width:] = v


def _cfm_sample_conv(cst, v, cw, cb, *, tok_block):
    ntok, flat = cst.shape
    width = v.shape[1]
    return pl.pallas_call(
        _cfm_sample_conv_kernel,
        grid=(ntok // tok_block,),
        in_specs=[
            pl.BlockSpec((tok_block, flat), lambda i: (i, 0)),
            pl.BlockSpec((tok_block, width), lambda i: (i, 0)),
            _const_spec(cw.shape),
            _const_spec(cb.shape),
        ],
        out_specs=[
            pl.BlockSpec((tok_block, width), lambda i: (i, 0)),
            pl.BlockSpec((tok_block, flat), lambda i: (i, 0)),
        ],
        out_shape=[
            jax.ShapeDtypeStruct((ntok, width), F32),
            jax.ShapeDtypeStruct((ntok, flat), F32),
        ],
        compiler_params=pltpu.CompilerParams(
            dimension_semantics=("arbitrary",),
            vmem_limit_bytes=VMEM_LIMIT_BYTES),
        name="cfm_sample_conv",
    )(cst, v, cw, cb)


def _cfm_sample_tail_kernel(c_ref, z_ref, x_ref, lng_ref, lnb_ref, wout_ref, gpost_ref, o_ref):
    c = _silu(_layer_norm(c_ref[...], lng_ref[...], lnb_ref[...]))
    o = _dot((c * _silu(z_ref[...])).astype(BF16), wout_ref[...])
    o_ref[...] = x_ref[...] + _rms(o, gpost_ref[...])


def _row(v):
    return v.reshape(1, -1)


def _pad_lanes(a):
    pad = (-a.shape[-1]) % LANES
    return jnp.pad(a, [(0, 0)] * (a.ndim - 1) + [(0, pad)])


def kernel(x_prompt, x_sample, state_ssm, state_conv_ssm, state_conv_cfm, g_pre, g_post, ssm_w_in, ssm_conv_w, ssm_conv_b, ssm_dt_bias, ssm_A_log, ssm_D, ssm_norm_g, ssm_w_out, cfm_w_in, cfm_b_in, cfm_conv_w, cfm_conv_b, cfm_ln_g, cfm_ln_b, cfm_w_out):
    d_inner = ssm_w_out.shape[1]
    conv_dim = ssm_conv_w.shape[2]
    heads = ssm_D.shape[1]
    width = cfm_w_out.shape[1]
    ntok = x_sample.shape[0]
    d = x_sample.shape[2]

    w_in = ssm_w_in[0]
    wz = w_in[:, :d_inner].astype(BF16)
    wxbc = w_in[:, d_inner:d_inner + conv_dim].astype(BF16)
    wdt = _pad_lanes(w_in[:, d_inner + conv_dim:]).astype(BF16)
    cw0, cb0 = ssm_conv_w[0], _row(ssm_conv_b[0])
    dtb = _pad_lanes(_row(ssm_dt_bias[0]))
    alog = _pad_lanes(_row(ssm_A_log[0]))
    dexp = _row(jnp.repeat(ssm_D[0], d_inner // heads))
    ng = _row(ssm_norm_g[0])
    wout0 = ssm_w_out[0].astype(BF16)
    gpre0, gpost0 = _row(g_pre[0]), _row(g_post[0])

    w1 = cfm_w_in[0]
    wv, wg, wz1 = (w1[:, i * width:(i + 1) * width].astype(BF16) for i in range(3))
    bv, bg, bz = (_row(cfm_b_in[0][i * width:(i + 1) * width]) for i in range(3))
    cw1, cb1 = cfm_conv_w[0], _row(cfm_conv_b[0])
    lng, lnb = _row(cfm_ln_g[0]), _row(cfm_ln_b[0])
    wout1 = cfm_w_out[0].astype(BF16)
    gpre1, gpost1 = _row(g_pre[1]), _row(g_post[1])

    xp1, ssm_p, cssm_p = _ssd_prompt_layer(
        x_prompt, gpre0, wz, wxbc, wdt, cw0, cb0, dtb, alog, dexp, ng, wout0, gpost0, tile=256)
    xp2, ccfm_p = _cfm_prompt_layer(
        xp1, gpre1, wv, wg, wz1, bv, bg, bz, cw1, cb1, lng, lnb, wout1, gpost1, tile=256)

    xs0 = x_sample.reshape(ntok, d)
    kw0 = cw0.shape[0]
    cst0 = state_conv_ssm[0].reshape(ntok, (kw0 - 1) * conv_dim)
    gn = N_GROUPS * D_STATE
    sds = jax.ShapeDtypeStruct
    z_s, xs_s, b_s, c_s, xdt_t, da_t, cst0_new = _single_step_call(
        _ssd_sample_front_kernel,
        [sds((ntok, d_inner), F32), sds((ntok, d_inner), F32), sds((ntok, gn), F32),
         sds((ntok, gn), F32), sds((d_inner, ntok), F32), sds((d_inner, ntok), F32),
         sds(cst0.shape, F32)],
        xs0, gpre0, wz, wxbc, wdt, cst0, cw0, cb0, dtb, alog,
        name="ssd_sample_front")
    h0 = state_ssm[0].reshape(ntok, heads * HEAD_DIM, D_STATE)
    h_new, y_t = _ssd_sample_state(
        h0, xdt_t, da_t, b_s.reshape(ntok, N_GROUPS, D_STATE), c_s.reshape(ntok, N_GROUPS, D_STATE))
    (xs1,) = _single_step_call(
        _ssd_sample_tail_kernel, [sds((ntok, d), F32)],
        y_t, xs_s, z_s, xs0, dexp, ng, wout0, gpost0, name="ssd_sample_tail")
    v_s, zc_s = _single_step_call(
        _cfm_sample_front_kernel, [sds((ntok, width), F32), sds((ntok, width), F32)],
        xs1, gpre1, wv, wg, wz1, bv, bg, bz, name="cfm_sample_front")
    kw1 = cw1.shape[0]
    cst1 = state_conv_cfm[0].reshape(ntok, (kw1 - 1) * width)
    c_conv, cst1_new = _cfm_sample_conv(cst1, v_s, cw1, cb1, tok_block=16)
    (xs2,) = _single_step_call(
        _cfm_sample_tail_kernel, [sds((ntok, d), F32)],
        c_conv, zc_s, xs1, lng, lnb, wout1, gpost1, name="cfm_sample_tail")

    return (xp2,
            xs2.reshape(x_sample.shape),
            ssm_p[None],
            cssm_p[None],
            ccfm_p[None],
            h_new.reshape(state_ssm.shape),
            cst0_new.reshape(state_conv_ssm.shape),
            cst1_new.reshape(state_conv_cfm.shape))
```

```python
import functools

import jax
import jax.numpy as jnp
from jax import lax
from jax.experimental import pallas as pl
from jax.experimental.pallas import tpu as pltpu

F32 = jnp.float32
BF16 = jnp.bfloat16

EPS = 1e-6
LANES = 128
CHUNK = 128
HEAD_DIM = 64
D_STATE = 128
N_GROUPS = 4
HEADS_PER_GROUP = 8
CFM_HALO_SLABS = 32
VMEM_LIMIT_BYTES = 56 * 1024 * 1024


def _dot(a, b):
    return jnp.dot(a, b, preferred_element_type=F32)


def _dot_nt(a, b):
    return lax.dot_general(a, b, (((1,), (1,)), ((), ())), preferred_element_type=F32)


def _dot_tn(a, b):
    return lax.dot_general(a, b, (((0,), (0,)), ((), ())), preferred_element_type=F32)


def _rms(x, g):
    return x * lax.rsqrt(jnp.mean(x * x, axis=-1, keepdims=True) + EPS) * g


def _layer_norm(x, g, b):
    mu = jnp.mean(x, axis=-1, keepdims=True)
    xc = x - mu
    return xc * lax.rsqrt(jnp.mean(xc * xc, axis=-1, keepdims=True) + EPS) * g + b


def _sigmoid(x):
    return 1.0 / (1.0 + jnp.exp(-x))


def _silu(x):
    return x * _sigmoid(x)


def _softplus(x):
    return jnp.maximum(x, 0.0) + jnp.log1p(jnp.exp(-jnp.abs(x)))


def _dot_f32_rhs(l_bf16, a):
    hi = a.astype(BF16)
    r1 = a - hi.astype(F32)
    mid = r1.astype(BF16)
    lo = (r1 - mid.astype(F32)).astype(BF16)
    return _dot(l_bf16, hi) + _dot(l_bf16, mid) + _dot(l_bf16, lo)


def _dot_f32_lhs(a, r_bf16):
    hi = a.astype(BF16)
    r1 = a - hi.astype(F32)
    mid = r1.astype(BF16)
    lo = (r1 - mid.astype(F32)).astype(BF16)
    return _dot(hi, r_bf16) + _dot(mid, r_bf16) + _dot(lo, r_bf16)


def _col(a, j):
    return a[:, j:j + 1]


def _ssd_prompt_kernel(x_ref, gpre_ref, wz_ref, wxbc_ref, wdt_ref, cw_ref, cb_ref,
                       dtb_ref, alog_ref, dexp_ref, ng_ref, wout_ref, gpost_ref,
                       y_ref, hout_ref, cout_ref,
                       ht_s, ext_s, xs_s, xdt_s, b_s, c_s, z_s, a_s, yb_s):
    t = pl.program_id(1)
    nt = pl.num_programs(1)
    tile = x_ref.shape[0]
    n_chunks = tile // CHUNK
    d_inner = xs_s.shape[1]
    gn = N_GROUPS * D_STATE
    kw = cw_ref.shape[0]
    halo = 8

    @pl.when(t == 0)
    def _():
        ht_s[...] = jnp.zeros_like(ht_s)
        ext_s[0:halo, :] = jnp.zeros((halo, ext_s.shape[1]), F32)

    x = x_ref[...]
    u = _rms(x, gpre_ref[...]).astype(BF16)
    z_s[...] = _dot(u, wz_ref[...])
    xbc = _dot(u, wxbc_ref[...])
    dt_raw = _dot(u, wdt_ref[...])

    ext_s[halo:halo + tile, :] = xbc
    acc = jnp.broadcast_to(cb_ref[...], xbc.shape)
    for k in range(kw):
        acc = acc + cw_ref[k:k + 1, :] * ext_s[pl.ds(halo - (kw - 1) + k, tile), :]
    cout_ref[...] = ext_s[pl.ds(halo + tile - (kw - 1), kw - 1), :]
    ext_s[0:halo, :] = xbc[tile - halo:, :]
    act = _silu(acc)
    xs = act[:, :d_inner]
    xs_s[...] = xs
    b_s[...] = act[:, d_inner:d_inner + gn].astype(BF16)
    c_s[...] = act[:, d_inner + gn:].astype(BF16)
    dt = _softplus(dt_raw + dtb_ref[...])
    a_s[...] = dt * (-jnp.exp(alog_ref[...]))
    hrow = lax.broadcasted_iota(jnp.int32, (LANES, d_inner), 0)
    ccol = lax.broadcasted_iota(jnp.int32, (LANES, d_inner), 1)
    expand = ((ccol >> 6) == hrow).astype(BF16)
    xdt_s[...] = xs * _dot_f32_lhs(dt, expand)

    row = lax.broadcasted_iota(jnp.int32, (CHUNK, CHUNK), 0)
    lane = lax.broadcasted_iota(jnp.int32, (CHUNK, CHUNK), 1)
    causal = row >= lane
    tri = causal.astype(BF16)
    lo_half = lane < HEAD_DIM
    lo_half_row = lo_half[0:1, :]
    dexp = dexp_ref[...]

    def chunk_body(c, carry):
        r0 = pl.multiple_of(c * CHUNK, CHUNK)
        rows = pl.ds(r0, CHUNK)
        a_c = a_s[rows, :]
        a_cum = _dot_f32_rhs(tri, a_c)
        a_cum_t = a_cum.T
        a_last = a_cum[CHUNK - 1:CHUNK, :]
        for g in range(N_GROUPS):
            gl = slice(g * D_STATE, (g + 1) * D_STATE)
            cg = c_s[rows, gl]
            bg = b_s[rows, gl]
            cb = _dot_nt(cg, bg)
            ht_g = ht_s[g]
            yoff_g = _dot(cg, ht_g.astype(BF16))
            xw_parts = []
            dec_parts = []
            for j in range(HEADS_PER_GROUP // 2):
                h0 = g * HEADS_PER_GROUP + 2 * j
                h1 = h0 + 1
                pl_ = slice((g * 4 + j) * LANES, (g * 4 + j + 1) * LANES)
                xdt_p = xdt_s[rows, pl_]
                acol0 = jnp.broadcast_to(_col(a_cum, h0), (CHUNK, CHUNK))
                acol1 = jnp.broadcast_to(_col(a_cum, h1), (CHUNK, CHUNK))

                def scores(h, acol):
                    diff = acol - a_cum_t[h:h + 1, :]
                    dec = jnp.exp(jnp.where(causal, diff, -jnp.inf))
                    return (cb * dec).astype(BF16)

                x0 = jnp.where(lo_half, xdt_p, 0.0).astype(BF16)
                x1 = jnp.where(lo_half, 0.0, xdt_p).astype(BF16)
                y_diag = _dot(scores(h0, acol0), x0) + _dot(scores(h1, acol1), x1)
                a_pair = jnp.where(lo_half, acol0, acol1)
                last_pair = jnp.where(lo_half_row, a_last[:, h0:h0 + 1], a_last[:, h1:h1 + 1])
                y_pair = (y_diag + jnp.exp(a_pair) * yoff_g[:, j * LANES:(j + 1) * LANES]
                          + dexp[:, pl_] * xs_s[rows, pl_])
                yb_s[rows, pl_] = y_pair
                xw_parts.append((xdt_p * jnp.exp(last_pair - a_pair)).astype(BF16))
                dec_parts.append(jnp.exp(last_pair))
            xw_g = jnp.concatenate(xw_parts, axis=1)
            dec_g = jnp.concatenate(dec_parts, axis=1)
            ht_s[g] = ht_g * dec_g + _dot_tn(bg, xw_g)
        return carry

    lax.fori_loop(0, n_chunks, chunk_body, 0)

    yg = yb_s[...] * _silu(z_s[...])
    yn = _rms(yg, ng_ref[...]).astype(BF16)
    o = _dot(yn, wout_ref[...])
    y_ref[...] = x + _rms(o, gpost_ref[...])

    @pl.when(t == nt - 1)
    def _():
        for g in range(N_GROUPS):
            hg = ht_s[g].T
            for r in range(HEADS_PER_GROUP):
                hout_ref[g * HEADS_PER_GROUP + r] = hg[r * HEAD_DIM:(r + 1) * HEAD_DIM, :]


def _const_spec(shape):
    nd = len(shape)
    return pl.BlockSpec(shape, lambda *_: (0,) * nd, pipeline_mode=pl.Buffered(1))


def _ssd_prompt_layer(x, gpre, wz, wxbc, wdt, cw, cb, dtb, alog, dexp, ng, wout, gpost, *, tile):
    bsz, seq, d = x.shape
    d_inner = wz.shape[1]
    conv_dim = wxbc.shape[1]
    gn = N_GROUPS * D_STATE
    heads = d_inner // HEAD_DIM
    kw = cw.shape[0]
    consts = (gpre, wz, wxbc, wdt, cw, cb, dtb, alog, dexp, ng, wout, gpost)
    return pl.pallas_call(
        _ssd_prompt_kernel,
        grid=(bsz, seq // tile),
        in_specs=[pl.BlockSpec((None, tile, d), lambda b, t: (b, t, 0))]
        + [_const_spec(c.shape) for c in consts],
        out_specs=[
            pl.BlockSpec((None, tile, d), lambda b, t: (b, t, 0)),
            pl.BlockSpec((None, heads, HEAD_DIM, D_STATE), lambda b, t: (b, 0, 0, 0)),
            pl.BlockSpec((None, kw - 1, conv_dim), lambda b, t: (b, 0, 0)),
        ],
        out_shape=[
            jax.ShapeDtypeStruct((bsz, seq, d), F32),
            jax.ShapeDtypeStruct((bsz, heads, HEAD_DIM, D_STATE), F32),
            jax.ShapeDtypeStruct((bsz, kw - 1, conv_dim), F32),
        ],
        scratch_shapes=[
            pltpu.VMEM((N_GROUPS, D_STATE, HEADS_PER_GROUP * HEAD_DIM), F32),
            pltpu.VMEM((8 + tile, conv_dim), F32),
            pltpu.VMEM((tile, d_inner), F32),
            pltpu.VMEM((tile, d_inner), F32),
            pltpu.VMEM((tile, gn), BF16),
            pltpu.VMEM((tile, gn), BF16),
            pltpu.VMEM((tile, d_inner), F32),
            pltpu.VMEM((tile, LANES), F32),
            pltpu.VMEM((tile, d_inner), F32),
        ],
        compiler_params=pltpu.CompilerParams(
            dimension_semantics=("arbitrary", "arbitrary"),
            vmem_limit_bytes=VMEM_LIMIT_BYTES),
        name="ssd_prompt_layer",
    )(x, *consts)


def _cfm_prompt_kernel(x_ref, gpre_ref, wv_ref, wg_ref, wz_ref, bv_ref, bg_ref, bz_ref,
                       cw_ref, cb_ref, lng_ref, lnb_ref, wout_ref, gpost_ref,
                       y_ref, cout_ref, ext_s):
    t = pl.program_id(1)
    nt = pl.num_programs(1)
    tile = x_ref.shape[0]
    width = ext_s.shape[1]
    kw = cw_ref.shape[0]
    nslab = tile // 8
    nh = kw - 1
    hb = CFM_HALO_SLABS * 8

    @pl.when(t == 0)
    def _():
        ext_s[hb:hb + tile, :] = jnp.zeros((tile, width), F32)

    x = x_ref[...]
    u = _rms(x, gpre_ref[...]).astype(BF16)
    ri = lax.broadcasted_iota(jnp.int32, (tile, tile), 0)
    ci = lax.broadcasted_iota(jnp.int32, (tile, tile), 1)
    to_perm = (ci == (ri & 7) * nslab + (ri >> 3)).astype(BF16)
    to_nat = (ri == (ci & 7) * nslab + (ci >> 3)).astype(BF16)
    u_p = _dot(to_perm, u).astype(BF16)
    val = _dot(u_p, wv_ref[...]) + bv_ref[...]
    gate = _dot(u_p, wg_ref[...]) + bg_ref[...]
    z = _dot(u_p, wz_ref[...]) + bz_ref[...]
    v = val * _sigmoid(gate)

    sub = lax.broadcasted_iota(jnp.int32, (nh * 8, width), 0) & 7
    prev_tail = ext_s[hb + (nslab - nh) * 8:hb + tile, :]
    mixed = jnp.where(sub == 7, prev_tail, v[(nslab - nh) * 8:, :]).reshape(nh, 8, width)
    halo = jnp.concatenate([mixed[:, 7:8, :], mixed[:, 0:7, :]], axis=1)
    ext_s[hb - nh * 8:hb, :] = halo.reshape(nh * 8, width)
    ext_s[hb:hb + tile, :] = v
    acc = jnp.broadcast_to(cb_ref[...], v.shape)
    for k in range(kw):
        acc = acc + cw_ref[k:k + 1, :] * ext_s[pl.ds((CFM_HALO_SLABS - nh + k) * 8, tile), :]

    @pl.when(t == nt - 1)
    def _():
        rr = lax.broadcasted_iota(jnp.int32, (CFM_HALO_SLABS, tile), 0)
        cc = lax.broadcasted_iota(jnp.int32, (CFM_HALO_SLABS, tile), 1)
        pick = (cc == (nslab - CFM_HALO_SLABS + rr) * 8 + 7).astype(BF16)
        cout_ref[...] = _dot_f32_rhs(pick, v)[CFM_HALO_SLABS - nh:, :]

    c = _silu(_layer_norm(acc, lng_ref[...], lnb_ref[...]))
    m_p = (c * _silu(z)).astype(BF16)
    m = _dot(to_nat, m_p).astype(BF16)
    o = _dot(m, wout_ref[...])
    y_ref[...] = x + _rms(o, gpost_ref[...])


def _cfm_prompt_layer(x, gpre, wv, wg, wz, bv, bg, bz, cw, cb, lng, lnb, wout, gpost, *, tile):
    bsz, seq, d = x.shape
    width = wv.shape[1]
    kw = cw.shape[0]
    assert kw - 1 <= CFM_HALO_SLABS <= tile // 8 and seq % tile == 0
    consts = (gpre, wv, wg, wz, bv, bg, bz, cw, cb, lng, lnb, wout, gpost)
    return pl.pallas_call(
        _cfm_prompt_kernel,
        grid=(bsz, seq // tile),
        in_specs=[pl.BlockSpec((None, tile, d), lambda b, t: (b, t, 0))]
        + [_const_spec(c.shape) for c in consts],
        out_specs=[
            pl.BlockSpec((None, tile, d), lambda b, t: (b, t, 0)),
            pl.BlockSpec((None, kw - 1, width), lambda b, t: (b, 0, 0)),
        ],
        out_shape=[
            jax.ShapeDtypeStruct((bsz, seq, d), F32),
            jax.ShapeDtypeStruct((bsz, kw - 1, width), F32),
        ],
        scratch_shapes=[pltpu.VMEM((CFM_HALO_SLABS * 8 + tile, width), F32)],
        compiler_params=pltpu.CompilerParams(
            dimension_semantics=("arbitrary", "arbitrary"),
            vmem_limit_bytes=VMEM_LIMIT_BYTES),
        name="cfm_prompt_layer",
    )(x, *consts)


def _single_step_call(kernel, out_shapes, *args, name):
    return pl.pallas_call(
        kernel,
        grid=(1,),
        in_specs=[_const_spec(a.shape) for a in args],
        out_specs=[pl.BlockSpec(s.shape, functools.partial(lambda nd, i: (0,) * nd, len(s.shape)))
                   for s in out_shapes],
        out_shape=out_shapes,
        compiler_params=pltpu.CompilerParams(
            dimension_semantics=("arbitrary",),
            vmem_limit_bytes=VMEM_LIMIT_BYTES),
        name=name,
    )(*args)


def _ssd_sample_front_kernel(x_ref, gpre_ref, wz_ref, wxbc_ref, wdt_ref, cst_ref, cw_ref, cb_ref,
                             dtb_ref, alog_ref,
                             z_ref, xs_ref, b_ref, c_ref, xdt_t_ref, da_t_ref, cst_out_ref):
    d_inner = xs_ref.shape[1]
    conv_dim = wxbc_ref.shape[1]
    gn = N_GROUPS * D_STATE
    kw = cw_ref.shape[0]
    u = _rms(x_ref[...], gpre_ref[...]).astype(BF16)
    z_ref[...] = _dot(u, wz_ref[...])
    xbc = _dot(u, wxbc_ref[...])
    dt_raw = _dot(u, wdt_ref[...])

    acc = cb_ref[...] + cw_ref[kw - 1:kw, :] * xbc
    for k in range(kw - 1):
        acc = acc + cw_ref[k:k + 1, :] * cst_ref[:, k * conv_dim:(k + 1) * conv_dim]
    cst_out_ref[:, :(kw - 2) * conv_dim] = cst_ref[:, conv_dim:]
    cst_out_ref[:, (kw - 2) * conv_dim:] = xbc
    act = _silu(acc)
    xs = act[:, :d_inner]
    xs_ref[...] = xs
    b_ref[...] = act[:, d_inner:d_inner + gn]
    c_ref[...] = act[:, d_inner + gn:]

    dt = _softplus(dt_raw + dtb_ref[...])
    a = dt * (-jnp.exp(alog_ref[...]))
    hrow = lax.broadcasted_iota(jnp.int32, (LANES, d_inner), 0)
    ccol = lax.broadcasted_iota(jnp.int32, (LANES, d_inner), 1)
    expand = ((ccol >> 6) == hrow).astype(BF16)
    dt_x = _dot_f32_lhs(dt, expand)
    a_x = _dot_f32_lhs(a, expand)
    xdt_t_ref[...] = (xs * dt_x).T
    da_t_ref[...] = jnp.exp(a_x).T


def _ssd_sample_state_kernel(h_ref, xdt_t_ref, da_t_ref, b_ref, c_ref, hn_ref, yt_ref):
    tok = pl.program_id(0)
    rows_per_group = HEADS_PER_GROUP * HEAD_DIM
    lane = lax.broadcasted_iota(jnp.int32, xdt_t_ref.shape, 1)
    sel = lane == tok
    xcol = jnp.sum(jnp.where(sel, xdt_t_ref[...], 0.0), axis=1, keepdims=True)
    acol = jnp.sum(jnp.where(sel, da_t_ref[...], 0.0), axis=1, keepdims=True)
    bb = b_ref[...]
    cc = c_ref[...]
    b_x = jnp.concatenate(
        [jnp.broadcast_to(bb[g:g + 1, :], (rows_per_group, D_STATE)) for g in range(N_GROUPS)], axis=0)
    c_x = jnp.concatenate(
        [jnp.broadcast_to(cc[g:g + 1, :], (rows_per_group, D_STATE)) for g in range(N_GROUPS)], axis=0)
    hn = acol * h_ref[...] + xcol * b_x
    hn_ref[...] = hn
    ycol = jnp.sum(hn * c_x, axis=1, keepdims=True)

    @pl.when(tok == 0)
    def _():
        yt_ref[...] = jnp.zeros_like(yt_ref)

    yt_ref[...] = jnp.where(sel, ycol, yt_ref[...])


def _ssd_sample_state(h0, xdt_t, da_t, b3, c3):
    ntok, rows, n = h0.shape
    return pl.pallas_call(
        _ssd_sample_state_kernel,
        grid=(ntok,),
        in_specs=[
            pl.BlockSpec((None, rows, n), lambda i: (i, 0, 0)),
            _const_spec(xdt_t.shape),
            _const_spec(da_t.shape),
            pl.BlockSpec((None,) + b3.shape[1:], lambda i: (i, 0, 0)),
            pl.BlockSpec((None,) + c3.shape[1:], lambda i: (i, 0, 0)),
        ],
        out_specs=[
            pl.BlockSpec((None, rows, n), lambda i: (i, 0, 0)),
            pl.BlockSpec(xdt_t.shape, lambda i: (0, 0)),
        ],
        out_shape=[
            jax.ShapeDtypeStruct(h0.shape, F32),
            jax.ShapeDtypeStruct(xdt_t.shape, F32),
        ],
        compiler_params=pltpu.CompilerParams(
            dimension_semantics=("arbitrary",),
            vmem_limit_bytes=VMEM_LIMIT_BYTES),
        name="ssd_sample_state",
    )(h0, xdt_t, da_t, b3, c3)


def _ssd_sample_tail_kernel(yt_ref, xs_ref, z_ref, x_ref, dexp_ref, ng_ref, wout_ref, gpost_ref, o_ref):
    y = yt_ref[...].T + dexp_ref[...] * xs_ref[...]
    yg = y * _silu(z_ref[...])
    yn = _rms(yg, ng_ref[...]).astype(BF16)
    o = _dot(yn, wout_ref[...])
    o_ref[...] = x_ref[...] + _rms(o, gpost_ref[...])


def _cfm_sample_front_kernel(x_ref, gpre_ref, wv_ref, wg_ref, wz_ref, bv_ref, bg_ref, bz_ref,
                             v_ref, z_ref):
    u = _rms(x_ref[...], gpre_ref[...]).astype(BF16)
    val = _dot(u, wv_ref[...]) + bv_ref[...]
    gate = _dot(u, wg_ref[...]) + bg_ref[...]
    z_ref[...] = _dot(u, wz_ref[...]) + bz_ref[...]
    v_ref[...] = val * _sigmoid(gate)


def _cfm_sample_conv_kernel(cst_ref, v_ref, cw_ref, cb_ref, c_ref, cst_out_ref):
    width = v_ref.shape[1]
    kw = cw_ref.shape[0]
    v = v_ref[...]
    acc = cb_ref[...] + cw_ref[kw - 1:kw, :] * v
    for k in range(kw - 1):
        acc = acc + cw_ref[k:k + 1, :] * cst_ref[:, k * width:(k + 1) * width]
    c_ref[...] = acc
    cst_out_ref[:, :(kw - 2) * width] = cst_ref[:, width:]
    cst_out_ref[:, (kw - 2) * width:] = v


def _cfm_sample_conv(cst, v, cw, cb, *, tok_block):
    ntok, flat = cst.shape
    width = v.shape[1]
    return pl.pallas_call(
        _cfm_sample_conv_kernel,
        grid=(ntok // tok_block,),
        in_specs=[
            pl.BlockSpec((tok_block, flat), lambda i: (i, 0)),
            pl.BlockSpec((tok_block, width), lambda i: (i, 0)),
            _const_spec(cw.shape),
            _const_spec(cb.shape),
        ],
        out_specs=[
            pl.BlockSpec((tok_block, width), lambda i: (i, 0)),
            pl.BlockSpec((tok_block, flat), lambda i: (i, 0)),
        ],
        out_shape=[
            jax.ShapeDtypeStruct((ntok, width), F32),
            jax.ShapeDtypeStruct((ntok, flat), F32),
        ],
        compiler_params=pltpu.CompilerParams(
            dimension_semantics=("arbitrary",),
            vmem_limit_bytes=VMEM_LIMIT_BYTES),
        name="cfm_sample_conv",
    )(cst, v, cw, cb)


def _cfm_sample_tail_kernel(c_ref, z_ref, x_ref, lng_ref, lnb_ref, wout_ref, gpost_ref, o_ref):
    c = _silu(_layer_norm(c_ref[...], lng_ref[...], lnb_ref[...]))
    o = _dot((c * _silu(z_ref[...])).astype(BF16), wout_ref[...])
    o_ref[...] = x_ref[...] + _rms(o, gpost_ref[...])


def _row(v):
    return v.reshape(1, -1)


def _pad_lanes(a):
    pad = (-a.shape[-1]) % LANES
    return jnp.pad(a, [(0, 0)] * (a.ndim - 1) + [(0, pad)])


def kernel(x_prompt, x_sample, state_ssm, state_conv_ssm, state_conv_cfm, g_pre, g_post, ssm_w_in, ssm_conv_w, ssm_conv_b, ssm_dt_bias, ssm_A_log, ssm_D, ssm_norm_g, ssm_w_out, cfm_w_in, cfm_b_in, cfm_conv_w, cfm_conv_b, cfm_ln_g, cfm_ln_b, cfm_w_out):
    d_inner = ssm_w_out.shape[1]
    conv_dim = ssm_conv_w.shape[2]
    heads = ssm_D.shape[1]
    width = cfm_w_out.shape[1]
    ntok = x_sample.shape[0]
    d = x_sample.shape[2]

    w_in = ssm_w_in[0]
    wz = w_in[:, :d_inner].astype(BF16)
    wxbc = w_in[:, d_inner:d_inner + conv_dim].astype(BF16)
    wdt = _pad_lanes(w_in[:, d_inner + conv_dim:]).astype(BF16)
    cw0, cb0 = ssm_conv_w[0], _row(ssm_conv_b[0])
    dtb = _pad_lanes(_row(ssm_dt_bias[0]))
    alog = _pad_lanes(_row(ssm_A_log[0]))
    dexp = _row(jnp.repeat(ssm_D[0], d_inner // heads))
    ng = _row(ssm_norm_g[0])
    wout0 = ssm_w_out[0].astype(BF16)
    gpre0, gpost0 = _row(g_pre[0]), _row(g_post[0])

    w1 = cfm_w_in[0]
    wv, wg, wz1 = (w1[:, i * width:(i + 1) * width].astype(BF16) for i in range(3))
    bv, bg, bz = (_row(cfm_b_in[0][i * width:(i + 1) * width]) for i in range(3))
    cw1, cb1 = cfm_conv_w[0], _row(cfm_conv_b[0])
    lng, lnb = _row(cfm_ln_g[0]), _row(cfm_ln_b[0])
    wout1 = cfm_w_out[0].astype(BF16)
    gpre1, gpost1 = _row(g_pre[1]), _row(g_post[1])

    xp1, ssm_p, cssm_p = _ssd_prompt_layer(
        x_prompt, gpre0, wz, wxbc, wdt, cw0, cb0, dtb, alog, dexp, ng, wout0, gpost0, tile=256)
    xp2, ccfm_p = _cfm_prompt_layer(
        xp1, gpre1, wv, wg, wz1, bv, bg, bz, cw1, cb1, lng, lnb, wout1, gpost1, tile=256)

    xs0 = x_sample.reshape(ntok, d)
    kw0 = cw0.shape[0]
    cst0 = state_conv_ssm[0].reshape(ntok, (kw0 - 1) * conv_dim)
    gn = N_GROUPS * D_STATE
    sds = jax.ShapeDtypeStruct
    z_s, xs_s, b_s, c_s, xdt_t, da_t, cst0_new = _single_step_call(
        _ssd_sample_front_kernel,
        [sds((ntok, d_inner), F32), sds((ntok, d_inner), F32), sds((ntok, gn), F32),
         sds((ntok, gn), F32), sds((d_inner, ntok), F32), sds((d_inner, ntok), F32),
         sds(cst0.shape, F32)],
        xs0, gpre0, wz, wxbc, wdt, cst0, cw0, cb0, dtb, alog,
        name="ssd_sample_front")
    h0 = state_ssm[0].reshape(ntok, heads * HEAD_DIM, D_STATE)
    h_new, y_t = _ssd_sample_state(
        h0, xdt_t, da_t, b_s.reshape(ntok, N_GROUPS, D_STATE), c_s.reshape(ntok, N_GROUPS, D_STATE))
    (xs1,) = _single_step_call(
        _ssd_sample_tail_kernel, [sds((ntok, d), F32)],
        y_t, xs_s, z_s, xs0, dexp, ng, wout0, gpost0, name="ssd_sample_tail")
    v_s, zc_s = _single_step_call(
        _cfm_sample_front_kernel, [sds((ntok, width), F32), sds((ntok, width), F32)],
        xs1, gpre1, wv, wg, wz1, bv, bg, bz, name="cfm_sample_front")
    kw1 = cw1.shape[0]
    cst1 = state_conv_cfm[0].reshape(ntok, (kw1 - 1) * width)
    c_conv, cst1_new = _cfm_sample_conv(cst1, v_s, cw1, cb1, tok_block=16)
    (xs2,) = _single_step_call(
        _cfm_sample_tail_kernel, [sds((ntok, d), F32)],
        c_conv, zc_s, xs1, lng, lnb, wout1, gpost1, name="cfm_sample_tail")

    return (xp2,
            xs2.reshape(x_sample.shape),
            ssm_p[None],
            cssm_p[None],
            ccfm_p[None],
            h_new.reshape(state_ssm.shape),
            cst0_new.reshape(state_conv_ssm.shape),
            cst1_new.reshape(state_conv_cfm.shape))
```

```python
import jax
import jax.numpy as jnp
from jax import lax
from jax.experimental import pallas as pl
from jax.experimental.pallas import tpu as pltpu

F32 = jnp.float32
BF16 = jnp.bfloat16

EPS = 1e-6
LANES = 128
CHUNK = 128
HEAD_DIM = 64
D_STATE = 128
N_GROUPS = 4
HEADS_PER_GROUP = 8
CFM_HALO_SLABS = 32
VMEM_LIMIT_BYTES = 56 * 1024 * 1024


def _dot(a, b):
    return jnp.dot(a, b, preferred_element_type=F32)


def _dot_nt(a, b):
    return lax.dot_general(a, b, (((1,), (1,)), ((), ())), preferred_element_type=F32)


def _dot_tn(a, b):
    return lax.dot_general(a, b, (((0,), (0,)), ((), ())), preferred_element_type=F32)


def _rms(x, g):
    return x * lax.rsqrt(jnp.mean(x * x, axis=-1, keepdims=True) + EPS) * g


def _layer_norm(x, g, b):
    mu = jnp.mean(x, axis=-1, keepdims=True)
    xc = x - mu
    return xc * lax.rsqrt(jnp.mean(xc * xc, axis=-1, keepdims=True) + EPS) * g + b


def _sigmoid(x):
    return 1.0 / (1.0 + jnp.exp(-x))


def _silu(x):
    return x * _sigmoid(x)


def _softplus(x):
    return jnp.maximum(x, 0.0) + jnp.log1p(jnp.exp(-jnp.abs(x)))


def _dot_f32_rhs(l_bf16, a):
    hi = a.astype(BF16)
    r1 = a - hi.astype(F32)
    mid = r1.astype(BF16)
    lo = (r1 - mid.astype(F32)).astype(BF16)
    return _dot(l_bf16, hi) + _dot(l_bf16, mid) + _dot(l_bf16, lo)


def _dot_f32_lhs(a, r_bf16):
    hi = a.astype(BF16)
    r1 = a - hi.astype(F32)
    mid = r1.astype(BF16)
    lo = (r1 - mid.astype(F32)).astype(BF16)
    return _dot(hi, r_bf16) + _dot(mid, r_bf16) + _dot(lo, r_bf16)


def _col(a, j):
    return a[:, j:j + 1]


def _ssd_prompt_kernel(x_ref, gpre_ref, wz_ref, wxbc_ref, wdt_ref, cw_ref, cb_ref,
                       dtb_ref, alog_ref, dexp_ref, ng_ref, wout_ref, gpost_ref,
                       y_ref, hout_ref, cout_ref,
                       ht_s, ext_s, xs_s, xdt_s, b_s, c_s, z_s, a_s, yb_s):
    t = pl.program_id(1)
    nt = pl.num_programs(1)
    tile = x_ref.shape[0]
    n_chunks = tile // CHUNK
    d_inner = xs_s.shape[1]
    gn = N_GROUPS * D_STATE
    kw = cw_ref.shape[0]
    halo = 8

    @pl.when(t == 0)
    def _():
        ht_s[...] = jnp.zeros_like(ht_s)
        ext_s[0:halo, :] = jnp.zeros((halo, ext_s.shape[1]), F32)

    x = x_ref[...]
    u = _rms(x, gpre_ref[...]).astype(BF16)
    z_s[...] = _dot(u, wz_ref[...])
    xbc = _dot(u, wxbc_ref[...])
    dt_raw = _dot(u, wdt_ref[...])

    ext_s[halo:halo + tile, :] = xbc
    acc = jnp.broadcast_to(cb_ref[...], xbc.shape)
    for k in range(kw):
        acc = acc + cw_ref[k:k + 1, :] * ext_s[pl.ds(halo - (kw - 1) + k, tile), :]
    cout_ref[...] = ext_s[pl.ds(halo + tile - (kw - 1), kw - 1), :]
    ext_s[0:halo, :] = xbc[tile - halo:, :]
    act = _silu(acc)
    xs = act[:, :d_inner]
    xs_s[...] = xs
    b_s[...] = act[:, d_inner:d_inner + gn].astype(BF16)
    c_s[...] = act[:, d_inner + gn:].astype(BF16)
    dt = _softplus(dt_raw + dtb_ref[...])
    a_s[...] = dt * (-jnp.exp(alog_ref[...]))
    hrow = lax.broadcasted_iota(jnp.int32, (LANES, d_inner), 0)
    ccol = lax.broadcasted_iota(jnp.int32, (LANES, d_inner), 1)
    expand = ((ccol >> 6) == hrow).astype(BF16)
    xdt_s[...] = xs * _dot_f32_lhs(dt, expand)

    row = lax.broadcasted_iota(jnp.int32, (CHUNK, CHUNK), 0)
    lane = lax.broadcasted_iota(jnp.int32, (CHUNK, CHUNK), 1)
    causal = row >= lane
    tri = causal.astype(BF16)
    lo_half = lane < HEAD_DIM
    lo_half_row = lo_half[0:1, :]
    dexp = dexp_ref[...]

    def chunk_body(c, carry):
        r0 = pl.multiple_of(c * CHUNK, CHUNK)
        rows = pl.ds(r0, CHUNK)
        a_c = a_s[rows, :]
        a_cum = _dot_f32_rhs(tri, a_c)
        a_cum_t = a_cum.T
        a_last = a_cum[CHUNK - 1:CHUNK, :]
        for g in range(N_GROUPS):
            gl = slice(g * D_STATE, (g + 1) * D_STATE)
            cg = c_s[rows, gl]
            bg = b_s[rows, gl]
            cb = _dot_nt(cg, bg)
            ht_g = ht_s[g]
            yoff_g = _dot(cg, ht_g.astype(BF16))
            xw_parts = []
            dec_parts = []
            for j in range(HEADS_PER_GROUP // 2):
                h0 = g * HEADS_PER_GROUP + 2 * j
                h1 = h0 + 1
                pl_ = slice((g * 4 + j) * LANES, (g * 4 + j + 1) * LANES)
                xdt_p = xdt_s[rows, pl_]
                acol0 = jnp.broadcast_to(_col(a_cum, h0), (CHUNK, CHUNK))
                acol1 = jnp.broadcast_to(_col(a_cum, h1), (CHUNK, CHUNK))

                def scores(h, acol):
                    diff = acol - a_cum_t[h:h + 1, :]
                    dec = jnp.exp(jnp.where(causal, diff, -jnp.inf))
                    return (cb * dec).astype(BF16)

                x0 = jnp.where(lo_half, xdt_p, 0.0).astype(BF16)
                x1 = jnp.where(lo_half, 0.0, xdt_p).astype(BF16)
                y_diag = _dot(scores(h0, acol0), x0) + _dot(scores(h1, acol1), x1)
                a_pair = jnp.where(lo_half, acol0, acol1)
                last_pair = jnp.where(lo_half_row, a_last[:, h0:h0 + 1], a_last[:, h1:h1 + 1])
                y_pair = (y_diag + jnp.exp(a_pair) * yoff_g[:, j * LANES:(j + 1) * LANES]
                          + dexp[:, pl_] * xs_s[rows, pl_])
                yb_s[rows, pl_] = y_pair
                xw_parts.append((xdt_p * jnp.exp(last_pair - a_pair)).astype(BF16))
                dec_parts.append(jnp.exp(last_pair))
            xw_g = jnp.concatenate(xw_parts, axis=1)
            dec_g = jnp.concatenate(dec_parts, axis=1)
            ht_s[g] = ht_g * dec_g + _dot_tn(bg, xw_g)
        return carry

    lax.fori_loop(0, n_chunks, chunk_body, 0)

    yg = yb_s[...] * _silu(z_s[...])
    yn = _rms(yg, ng_ref[...]).astype(BF16)
    o = _dot(yn, wout_ref[...])
    y_ref[...] = x + _rms(o, gpost_ref[...])

    @pl.when(t == nt - 1)
    def _():
        for g in range(N_GROUPS):
            hg = ht_s[g].T
            for r in range(HEADS_PER_GROUP):
                hout_ref[g * HEADS_PER_GROUP + r] = hg[r * HEAD_DIM:(r + 1) * HEAD_DIM, :]


def _const_spec(shape):
    nd = len(shape)
    return pl.BlockSpec(shape, lambda *_: (0,) * nd, pipeline_mode=pl.Buffered(1))


def _ssd_prompt_layer(x, gpre, wz, wxbc, wdt, cw, cb, dtb, alog, dexp, ng, wout, gpost, *, tile):
    bsz, seq, d = x.shape
    d_inner = wz.shape[1]
    conv_dim = wxbc.shape[1]
    gn = N_GROUPS * D_STATE
    heads = d_inner // HEAD_DIM
    kw = cw.shape[0]
    consts = (gpre, wz, wxbc, wdt, cw, cb, dtb, alog, dexp, ng, wout, gpost)
    return pl.pallas_call(
        _ssd_prompt_kernel,
        grid=(bsz, seq // tile),
        in_specs=[pl.BlockSpec((None, tile, d), lambda b, t: (b, t, 0))]
        + [_const_spec(c.shape) for c in consts],
        out_specs=[
            pl.BlockSpec((None, tile, d), lambda b, t: (b, t, 0)),
            pl.BlockSpec((None, heads, HEAD_DIM, D_STATE), lambda b, t: (b, 0, 0, 0)),
            pl.BlockSpec((None, kw - 1, conv_dim), lambda b, t: (b, 0, 0)),
        ],
        out_shape=[
            jax.ShapeDtypeStruct((bsz, seq, d), F32),
            jax.ShapeDtypeStruct((bsz, heads, HEAD_DIM, D_STATE), F32),
            jax.ShapeDtypeStruct((bsz, kw - 1, conv_dim), F32),
        ],
        scratch_shapes=[
            pltpu.VMEM((N_GROUPS, D_STATE, HEADS_PER_GROUP * HEAD_DIM), F32),
            pltpu.VMEM((8 + tile, conv_dim), F32),
            pltpu.VMEM((tile, d_inner), F32),
            pltpu.VMEM((tile, d_inner), F32),
            pltpu.VMEM((tile, gn), BF16),
            pltpu.VMEM((tile, gn), BF16),
            pltpu.VMEM((tile, d_inner), F32),
            pltpu.VMEM((tile, LANES), F32),
            pltpu.VMEM((tile, d_inner), F32),
        ],
        compiler_params=pltpu.CompilerParams(
            dimension_semantics=("arbitrary", "arbitrary"),
            vmem_limit_bytes=VMEM_LIMIT_BYTES),
        name="ssd_prompt_layer",
    )(x, *consts)


def _cfm_prompt_kernel(x_ref, gpre_ref, wv_ref, wg_ref, wz_ref, bv_ref, bg_ref, bz_ref,
                       cw_ref, cb_ref, lng_ref, lnb_ref, wout_ref, gpost_ref,
                       y_ref, cout_ref, ext_s):
    t = pl.program_id(1)
    nt = pl.num_programs(1)
    tile = x_ref.shape[0]
    width = ext_s.shape[1]
    kw = cw_ref.shape[0]
    nslab = tile // 8
    nh = kw - 1
    hb = CFM_HALO_SLABS * 8

    @pl.when(t == 0)
    def _():
        ext_s[hb:hb + tile, :] = jnp.zeros((tile, width), F32)

    x = x_ref[...]
    u = _rms(x, gpre_ref[...]).astype(BF16)
    ri = lax.broadcasted_iota(jnp.int32, (tile, tile), 0)
    ci = lax.broadcasted_iota(jnp.int32, (tile, tile), 1)
    to_perm = (ci == (ri & 7) * nslab + (ri >> 3)).astype(BF16)
    to_nat = (ri == (ci & 7) * nslab + (ci >> 3)).astype(BF16)
    u_p = _dot(to_perm, u).astype(BF16)
    val = _dot(u_p, wv_ref[...]) + bv_ref[...]
    gate = _dot(u_p, wg_ref[...]) + bg_ref[...]
    z = _dot(u_p, wz_ref[...]) + bz_ref[...]
    v = val * _sigmoid(gate)

    sub = lax.broadcasted_iota(jnp.int32, (nh * 8, width), 0) & 7
    prev_tail = ext_s[hb + (nslab - nh) * 8:hb + tile, :]
    mixed = jnp.where(sub == 7, prev_tail, v[(nslab - nh) * 8:, :]).reshape(nh, 8, width)
    halo = jnp.concatenate([mixed[:, 7:8, :], mixed[:, 0:7, :]], axis=1)
    ext_s[hb - nh * 8:hb, :] = halo.reshape(nh * 8, width)
    ext_s[hb:hb + tile, :] = v
    acc = jnp.broadcast_to(cb_ref[...], v.shape)
    for k in range(kw):
        acc = acc + cw_ref[k:k + 1, :] * ext_s[pl.ds((CFM_HALO_SLABS - nh + k) * 8, tile), :]

    @pl.when(t == nt - 1)
    def _():
        rr = lax.broadcasted_iota(jnp.int32, (CFM_HALO_SLABS, tile), 0)
        cc = lax.broadcasted_iota(jnp.int32, (CFM_HALO_SLABS, tile), 1)
        pick = (cc == (nslab - CFM_HALO_SLABS + rr) * 8 + 7).astype(BF16)
        cout_ref[...] = _dot_f32_rhs(pick, v)[CFM_HALO_SLABS - nh:, :]

    c = _silu(_layer_norm(acc, lng_ref[...], lnb_ref[...]))
    m_p = (c * _silu(z)).astype(BF16)
    m = _dot(to_nat, m_p).astype(BF16)
    o = _dot(m, wout_ref[...])
    y_ref[...] = x + _rms(o, gpost_ref[...])


def _cfm_prompt_layer(x, gpre, wv, wg, wz, bv, bg, bz, cw, cb, lng, lnb, wout, gpost, *, tile):
    bsz, seq, d = x.shape
    width = wv.shape[1]
    kw = cw.shape[0]
    assert kw - 1 <= CFM_HALO_SLABS <= tile // 8 and seq % tile == 0
    consts = (gpre, wv, wg, wz, bv, bg, bz, cw, cb, lng, lnb, wout, gpost)
    return pl.pallas_call(
        _cfm_prompt_kernel,
        grid=(bsz, seq // tile),
        in_specs=[pl.BlockSpec((None, tile, d), lambda b, t: (b, t, 0))]
        + [_const_spec(c.shape) for c in consts],
        out_specs=[
            pl.BlockSpec((None, tile, d), lambda b, t: (b, t, 0)),
            pl.BlockSpec((None, kw - 1, width), lambda b, t: (b, 0, 0)),
        ],
        out_shape=[
            jax.ShapeDtypeStruct((bsz, seq, d), F32),
            jax.ShapeDtypeStruct((bsz, kw - 1, width), F32),
        ],
        scratch_shapes=[pltpu.VMEM((CFM_HALO_SLABS * 8 + tile, width), F32)],
        compiler_params=pltpu.CompilerParams(
            dimension_semantics=("arbitrary", "arbitrary"),
            vmem_limit_bytes=VMEM_LIMIT_BYTES),
        name="cfm_prompt_layer",
    )(x, *consts)


def _single_step_call(kernel, out_shapes, *args, name):
    return pl.pallas_call(
        kernel,
        grid=(1,),
        in_specs=[_const_spec(a.shape) for a in args],
        out_specs=[_const_spec(s.shape) for s in out_shapes],
        out_shape=out_shapes,
        compiler_params=pltpu.CompilerParams(
            dimension_semantics=("arbitrary",),
            vmem_limit_bytes=VMEM_LIMIT_BYTES),
        name=name,
    )(*args)


def _ssd_sample_front_kernel(x_ref, gpre_ref, wz_ref, wxbc_ref, wdt_ref, cst_ref, cw_ref, cb_ref,
                             dtb_ref, alog_ref,
                             z_ref, xs_ref, b_ref, c_ref, xdt_t_ref, da_ref, cst_out_ref):
    d_inner = xs_ref.shape[1]
    gn = N_GROUPS * D_STATE
    kw = cw_ref.shape[0]
    u = _rms(x_ref[...], gpre_ref[...]).astype(BF16)
    z_ref[...] = _dot(u, wz_ref[...])
    xbc = _dot(u, wxbc_ref[...])
    dt_raw = _dot(u, wdt_ref[...])

    acc = cb_ref[...] + cw_ref[kw - 1:kw, :] * xbc
    for k in range(kw - 1):
        acc = acc + cw_ref[k:k + 1, :] * cst_ref[k]
    for k in range(kw - 2):
        cst_out_ref[k] = cst_ref[k + 1]
    cst_out_ref[kw - 2] = xbc
    act = _silu(acc)
    xs = act[:, :d_inner]
    xs_ref[...] = xs
    b_ref[...] = act[:, d_inner:d_inner + gn]
    c_ref[...] = act[:, d_inner + gn:]

    dt = _softplus(dt_raw + dtb_ref[...])
    da_ref[...] = jnp.exp(dt * (-jnp.exp(alog_ref[...])))[:, :da_ref.shape[1]]
    hrow = lax.broadcasted_iota(jnp.int32, (LANES, d_inner), 0)
    ccol = lax.broadcasted_iota(jnp.int32, (LANES, d_inner), 1)
    expand = ((ccol >> 6) == hrow).astype(BF16)
    xdt_t_ref[...] = (xs * _dot_f32_lhs(dt, expand)).T.astype(BF16)


def _ssd_sample_state_kernel(da_ref, h_ref, xdt_t_ref, ball_ref, c_ref, hn_ref, yt_ref):
    step = pl.program_id(0)
    tok_block = h_ref.shape[0]
    ntok = ball_ref.shape[0]
    rpg = HEADS_PER_GROUP * HEAD_DIM
    tok_row = lax.broadcasted_iota(jnp.int32, (ntok, D_STATE), 0)
    tok_lane = lax.broadcasted_iota(jnp.int32, (rpg, ntok), 1)
    blk_row = lax.broadcasted_iota(jnp.int32, (tok_block, D_STATE), 0)

    @pl.when(step == 0)
    def _():
        yt_ref[...] = jnp.zeros_like(yt_ref)

    def token_body(j, carry):
        tok = step * tok_block + j
        for g in range(N_GROUPS):
            gl = slice(g * D_STATE, (g + 1) * D_STATE)
            gr = slice(g * rpg, (g + 1) * rpg)
            b_tok = jnp.where(tok_row == tok, ball_ref[:, gl], 0.0).astype(BF16)
            outer = _dot(xdt_t_ref[gr, :], b_tok)
            scaled = [h_ref[j, (g * HEADS_PER_GROUP + r) * HEAD_DIM:(g * HEADS_PER_GROUP + r + 1) * HEAD_DIM, :]
                      * da_ref[tok, g * HEADS_PER_GROUP + r] for r in range(HEADS_PER_GROUP)]
            hn_g = jnp.concatenate(scaled, axis=0) + outer
            hn_ref[j, gr, :] = hn_g
            c_tok = jnp.sum(jnp.where(blk_row == j, c_ref[:, gl], 0.0), axis=0, keepdims=True)
            ycol = jnp.sum(hn_g * c_tok, axis=1, keepdims=True)
            yt_ref[gr, :] = jnp.where(tok_lane == tok, ycol, yt_ref[gr, :])
        return carry

    lax.fori_loop(0, tok_block, token_body, 0)


def _ssd_sample_state(da, h0, xdt_t, b_all, c_all, *, tok_block):
    ntok, rows, n = h0.shape
    return pl.pallas_call(
        _ssd_sample_state_kernel,
        grid=(ntok // tok_block,),
        in_specs=[
            pl.BlockSpec(memory_space=pltpu.SMEM),
            pl.BlockSpec((tok_block, rows, n), lambda i: (i, 0, 0)),
            _const_spec(xdt_t.shape),
            _const_spec(b_all.shape),
            pl.BlockSpec((tok_block, c_all.shape[1]), lambda i: (i, 0)),
        ],
        out_specs=[
            pl.BlockSpec((tok_block, rows, n), lambda i: (i, 0, 0)),
            pl.BlockSpec((rows, ntok), lambda i: (0, 0)),
        ],
        out_shape=[
            jax.ShapeDtypeStruct(h0.shape, F32),
            jax.ShapeDtypeStruct((rows, ntok), F32),
        ],
        compiler_params=pltpu.CompilerParams(
            dimension_semantics=("arbitrary",),
            vmem_limit_bytes=VMEM_LIMIT_BYTES),
        name="ssd_sample_state",
    )(da, h0, xdt_t, b_all, c_all)


def _ssd_sample_tail_kernel(yt_ref, xs_ref, z_ref, x_ref, dexp_ref, ng_ref, wout_ref, gpost_ref, o_ref):
    y = yt_ref[...].T + dexp_ref[...] * xs_ref[...]
    yg = y * _silu(z_ref[...])
    yn = _rms(yg, ng_ref[...]).astype(BF16)
    o = _dot(yn, wout_ref[...])
    o_ref[...] = x_ref[...] + _rms(o, gpost_ref[...])


def _cfm_sample_front_kernel(x_ref, gpre_ref, wv_ref, wg_ref, wz_ref, bv_ref, bg_ref, bz_ref,
                             v_ref, z_ref):
    u = _rms(x_ref[...], gpre_ref[...]).astype(BF16)
    val = _dot(u, wv_ref[...]) + bv_ref[...]
    gate = _dot(u, wg_ref[...]) + bg_ref[...]
    z_ref[...] = _dot(u, wz_ref[...]) + bz_ref[...]
    v_ref[...] = val * _sigmoid(gate)


def _cfm_sample_conv_kernel(cst_ref, v_ref, cw_ref, cb_ref, c_ref, cst_out_ref):
    kw = cw_ref.shape[0]
    v = v_ref[...]
    acc = cb_ref[...] + cw_ref[kw - 1:kw, :] * v
    for k in range(kw - 1):
        acc = acc + cw_ref[k:k + 1, :] * cst_ref[k]
    c_ref[...] = acc
    for k in range(kw - 2):
        cst_out_ref[k] = cst_ref[k + 1]
    cst_out_ref[kw - 2] = v


def _cfm_sample_conv(cst, v, cw, cb, *, tok_block):
    taps, ntok, width = cst.shape
    return pl.pallas_call(
        _cfm_sample_conv_kernel,
        grid=(ntok // tok_block,),
        in_specs=[
            pl.BlockSpec((taps, tok_block, width), lambda i: (0, i, 0)),
            pl.BlockSpec((tok_block, width), lambda i: (i, 0)),
            _const_spec(cw.shape),
            _const_spec(cb.shape),
        ],
        out_specs=[
            pl.BlockSpec((tok_block, width), lambda i: (i, 0)),
            pl.BlockSpec((taps, tok_block, width), lambda i: (0, i, 0)),
        ],
        out_shape=[
            jax.ShapeDtypeStruct((ntok, width), F32),
            jax.ShapeDtypeStruct(cst.shape, F32),
        ],
        compiler_params=pltpu.CompilerParams(
            dimension_semantics=("arbitrary",),
            vmem_limit_bytes=VMEM_LIMIT_BYTES),
        name="cfm_sample_conv",
    )(cst, v, cw, cb)


def _cfm_sample_tail_kernel(c_ref, z_ref, x_ref, lng_ref, lnb_ref, wout_ref, gpost_ref, o_ref):
    c = _silu(_layer_norm(c_ref[...], lng_ref[...], lnb_ref[...]))
    o = _dot((c * _silu(z_ref[...])).astype(BF16), wout_ref[...])
    o_ref[...] = x_ref[...] + _rms(o, gpost_ref[...])


def _row(v):
    return v.reshape(1, -1)


def _pad_lanes(a):
    pad = (-a.shape[-1]) % LANES
    return jnp.pad(a, [(0, 0)] * (a.ndim - 1) + [(0, pad)])


def kernel(x_prompt, x_sample, state_ssm, state_conv_ssm, state_conv_cfm, g_pre, g_post, ssm_w_in, ssm_conv_w, ssm_conv_b, ssm_dt_bias, ssm_A_log, ssm_D, ssm_norm_g, ssm_w_out, cfm_w_in, cfm_b_in, cfm_conv_w, cfm_conv_b, cfm_ln_g, cfm_ln_b, cfm_w_out):
    d_inner = ssm_w_out.shape[1]
    conv_dim = ssm_conv_w.shape[2]
    heads = ssm_D.shape[1]
    width = cfm_w_out.shape[1]
    ntok = x_sample.shape[0]
    d = x_sample.shape[2]

    w_in = ssm_w_in[0]
    wz = w_in[:, :d_inner].astype(BF16)
    wxbc = w_in[:, d_inner:d_inner + conv_dim].astype(BF16)
    wdt = _pad_lanes(w_in[:, d_inner + conv_dim:]).astype(BF16)
    cw0, cb0 = ssm_conv_w[0], _row(ssm_conv_b[0])
    dtb = _pad_lanes(_row(ssm_dt_bias[0]))
    alog = _pad_lanes(_row(ssm_A_log[0]))
    dexp = _row(jnp.repeat(ssm_D[0], d_inner // heads))
    ng = _row(ssm_norm_g[0])
    wout0 = ssm_w_out[0].astype(BF16)
    gpre0, gpost0 = _row(g_pre[0]), _row(g_post[0])

    w1 = cfm_w_in[0]
    wv, wg, wz1 = (w1[:, i * width:(i + 1) * width].astype(BF16) for i in range(3))
    bv, bg, bz = (_row(cfm_b_in[0][i * width:(i + 1) * width]) for i in range(3))
    cw1, cb1 = cfm_conv_w[0], _row(cfm_conv_b[0])
    lng, lnb = _row(cfm_ln_g[0]), _row(cfm_ln_b[0])
    wout1 = cfm_w_out[0].astype(BF16)
    gpre1, gpost1 = _row(g_pre[1]), _row(g_post[1])

    xp1, ssm_p, cssm_p = _ssd_prompt_layer(
        x_prompt, gpre0, wz, wxbc, wdt, cw0, cb0, dtb, alog, dexp, ng, wout0, gpost0, tile=256)
    xp2, ccfm_p = _cfm_prompt_layer(
        xp1, gpre1, wv, wg, wz1, bv, bg, bz, cw1, cb1, lng, lnb, wout1, gpost1, tile=256)

    xs0 = x_sample.reshape(ntok, d)
    cst0 = jnp.transpose(state_conv_ssm[0], (1, 0, 2))
    gn = N_GROUPS * D_STATE
    sds = jax.ShapeDtypeStruct
    z_s, xs_s, b_s, c_s, xdt_t, da, cst0_new = _single_step_call(
        _ssd_sample_front_kernel,
        [sds((ntok, d_inner), F32), sds((ntok, d_inner), F32), sds((ntok, gn), F32),
         sds((ntok, gn), F32), sds((d_inner, ntok), BF16), sds((ntok, heads), F32),
         sds(cst0.shape, F32)],
        xs0, gpre0, wz, wxbc, wdt, cst0, cw0, cb0, dtb, alog,
        name="ssd_sample_front")
    h0 = state_ssm[0].reshape(ntok, heads * HEAD_DIM, D_STATE)
    h_new, y_t = _ssd_sample_state(da, h0, xdt_t, b_s, c_s, tok_block=8)
    (xs1,) = _single_step_call(
        _ssd_sample_tail_kernel, [sds((ntok, d), F32)],
        y_t, xs_s, z_s, xs0, dexp, ng, wout0, gpost0, name="ssd_sample_tail")
    v_s, zc_s = _single_step_call(
        _cfm_sample_front_kernel, [sds((ntok, width), F32), sds((ntok, width), F32)],
        xs1, gpre1, wv, wg, wz1, bv, bg, bz, name="cfm_sample_front")
    c_conv, cst1_new = _cfm_sample_conv(
        jnp.transpose(state_conv_cfm[0], (1, 0, 2)), v_s, cw1, cb1, tok_block=16)
    (xs2,) = _single_step_call(
        _cfm_sample_tail_kernel, [sds((ntok, d), F32)],
        c_conv, zc_s, xs1, lng, lnb, wout1, gpost1, name="cfm_sample_tail")

    return (xp2,
            xs2.reshape(x_sample.shape),
            ssm_p[None],
            cssm_p[None],
            ccfm_p[None],
            h_new.reshape(state_ssm.shape),
            jnp.transpose(cst0_new, (1, 0, 2))[None],
            jnp.transpose(cst1_new, (1, 0, 2))[None])
```

```python
import functools

import jax
import jax.numpy as jnp
from jax import lax
from jax.experimental import pallas as pl
from jax.experimental.pallas import tpu as pltpu

F32 = jnp.float32
BF16 = jnp.bfloat16

EPS = 1e-6
LANES = 128
CHUNK = 128
HEAD_DIM = 64
D_STATE = 128
N_GROUPS = 4
HEADS_PER_GROUP = 8
CFM_HALO_SLABS = 32
CFM_CBLK = 256
CFM_SLAB_GROUP = 8
VMEM_LIMIT_BYTES = 56 * 1024 * 1024


def _dot(a, b):
    return jnp.dot(a, b, preferred_element_type=F32)


def _dot_nt(a, b):
    return lax.dot_general(a, b, (((1,), (1,)), ((), ())), preferred_element_type=F32)


def _dot_tn(a, b):
    return lax.dot_general(a, b, (((0,), (0,)), ((), ())), preferred_element_type=F32)


def _rms(x, g):
    return x * lax.rsqrt(jnp.mean(x * x, axis=-1, keepdims=True) + EPS) * g


def _layer_norm(x, g, b):
    mu = jnp.mean(x, axis=-1, keepdims=True)
    xc = x - mu
    return xc * lax.rsqrt(jnp.mean(xc * xc, axis=-1, keepdims=True) + EPS) * g + b


def _sigmoid(x):
    return 1.0 / (1.0 + jnp.exp(-x))


def _silu(x):
    return x * _sigmoid(x)


def _softplus(x):
    return jnp.maximum(x, 0.0) + jnp.log1p(jnp.exp(-jnp.abs(x)))


def _dot_f32_rhs(l_bf16, a):
    hi = a.astype(BF16)
    r1 = a - hi.astype(F32)
    mid = r1.astype(BF16)
    lo = (r1 - mid.astype(F32)).astype(BF16)
    return _dot(l_bf16, hi) + _dot(l_bf16, mid) + _dot(l_bf16, lo)


def _dot_f32_lhs(a, r_bf16):
    hi = a.astype(BF16)
    r1 = a - hi.astype(F32)
    mid = r1.astype(BF16)
    lo = (r1 - mid.astype(F32)).astype(BF16)
    return _dot(hi, r_bf16) + _dot(mid, r_bf16) + _dot(lo, r_bf16)


def _col(a, j):
    return a[:, j:j + 1]


def _ssd_prompt_kernel(x_ref, gpre_ref, wz_ref, wxbc_ref, wdt_ref, cw_ref, cb_ref,
                       dtb_ref, alog_ref, dexp_ref, ng_ref, wout_ref, gpost_ref,
                       y_ref, hout_ref, cout_ref,
                       ht_s, ext_s, xs_s, xdt_s, b_s, c_s, z_s, a_s, yb_s):
    t = pl.program_id(1)
    nt = pl.num_programs(1)
    tile = x_ref.shape[0]
    n_chunks = tile // CHUNK
    d_inner = xs_s.shape[1]
    gn = N_GROUPS * D_STATE
    kw = cw_ref.shape[0]
    halo = 8

    @pl.when(t == 0)
    def _():
        ht_s[...] = jnp.zeros_like(ht_s)
        ext_s[0:halo, :] = jnp.zeros((halo, ext_s.shape[1]), F32)

    x = x_ref[...]
    u = _rms(x, gpre_ref[...]).astype(BF16)
    z_s[...] = _dot(u, wz_ref[...])
    xbc = _dot(u, wxbc_ref[...])
    dt_raw = _dot(u, wdt_ref[...])

    ext_s[halo:halo + tile, :] = xbc
    acc = jnp.broadcast_to(cb_ref[...], xbc.shape)
    for k in range(kw):
        acc = acc + cw_ref[k:k + 1, :] * ext_s[pl.ds(halo - (kw - 1) + k, tile), :]
    cout_ref[...] = ext_s[pl.ds(halo + tile - (kw - 1), kw - 1), :]
    ext_s[0:halo, :] = xbc[tile - halo:, :]
    act = _silu(acc)
    xs = act[:, :d_inner]
    xs_s[...] = xs
    b_s[...] = act[:, d_inner:d_inner + gn].astype(BF16)
    c_s[...] = act[:, d_inner + gn:].astype(BF16)
    dt = _softplus(dt_raw + dtb_ref[...])
    a_s[...] = dt * (-jnp.exp(alog_ref[...]))
    hrow = lax.broadcasted_iota(jnp.int32, (LANES, d_inner), 0)
    ccol = lax.broadcasted_iota(jnp.int32, (LANES, d_inner), 1)
    expand = ((ccol >> 6) == hrow).astype(BF16)
    xdt_s[...] = xs * _dot_f32_lhs(dt, expand)

    row = lax.broadcasted_iota(jnp.int32, (CHUNK, CHUNK), 0)
    lane = lax.broadcasted_iota(jnp.int32, (CHUNK, CHUNK), 1)
    causal = row >= lane
    tri = causal.astype(BF16)
    lo_half = lane < HEAD_DIM
    lo_half_row = lo_half[0:1, :]
    dexp = dexp_ref[...]

    def chunk_body(c, carry):
        r0 = pl.multiple_of(c * CHUNK, CHUNK)
        rows = pl.ds(r0, CHUNK)
        a_c = a_s[rows, :]
        a_cum = _dot_f32_rhs(tri, a_c)
        a_cum_t = a_cum.T
        a_last = a_cum[CHUNK - 1:CHUNK, :]
        for g in range(N_GROUPS):
            gl = slice(g * D_STATE, (g + 1) * D_STATE)
            cg = c_s[rows, gl]
            bg = b_s[rows, gl]
            cb = _dot_nt(cg, bg)
            ht_g = ht_s[g]
            yoff_g = _dot(cg, ht_g.astype(BF16))
            xw_parts = []
            dec_parts = []
            for j in range(HEADS_PER_GROUP // 2):
                h0 = g * HEADS_PER_GROUP + 2 * j
                h1 = h0 + 1
                pl_ = slice((g * 4 + j) * LANES, (g * 4 + j + 1) * LANES)
                xdt_p = xdt_s[rows, pl_]
                acol0 = jnp.broadcast_to(_col(a_cum, h0), (CHUNK, CHUNK))
                acol1 = jnp.broadcast_to(_col(a_cum, h1), (CHUNK, CHUNK))

                def scores(h, acol):
                    diff = acol - a_cum_t[h:h + 1, :]
                    dec = jnp.exp(jnp.where(causal, diff, -jnp.inf))
                    return (cb * dec).astype(BF16)

                x0 = jnp.where(lo_half, xdt_p, 0.0).astype(BF16)
                x1 = jnp.where(lo_half, 0.0, xdt_p).astype(BF16)
                y_diag = _dot(scores(h0, acol0), x0) + _dot(scores(h1, acol1), x1)
                a_pair = jnp.where(lo_half, acol0, acol1)
                last_pair = jnp.where(lo_half_row, a_last[:, h0:h0 + 1], a_last[:, h1:h1 + 1])
                y_pair = (y_diag + jnp.exp(a_pair) * yoff_g[:, j * LANES:(j + 1) * LANES]
                          + dexp[:, pl_] * xs_s[rows, pl_])
                yb_s[rows, pl_] = y_pair
                xw_parts.append((xdt_p * jnp.exp(last_pair - a_pair)).astype(BF16))
                dec_parts.append(jnp.exp(last_pair))
            xw_g = jnp.concatenate(xw_parts, axis=1)
            dec_g = jnp.concatenate(dec_parts, axis=1)
            ht_s[g] = ht_g * dec_g + _dot_tn(bg, xw_g)
        return carry

    lax.fori_loop(0, n_chunks, chunk_body, 0)

    yg = yb_s[...] * _silu(z_s[...])
    yn = _rms(yg, ng_ref[...]).astype(BF16)
    o = _dot(yn, wout_ref[...])
    y_ref[...] = x + _rms(o, gpost_ref[...])

    @pl.when(t == nt - 1)
    def _():
        for g in range(N_GROUPS):
            hg = ht_s[g].T
            for r in range(HEADS_PER_GROUP):
                hout_ref[g * HEADS_PER_GROUP + r] = hg[r * HEAD_DIM:(r + 1) * HEAD_DIM, :]


def _const_spec(shape):
    nd = len(shape)
    return pl.BlockSpec(shape, lambda *_: (0,) * nd, pipeline_mode=pl.Buffered(1))


def _ssd_prompt_layer(x, gpre, wz, wxbc, wdt, cw, cb, dtb, alog, dexp, ng, wout, gpost, *, tile):
    bsz, seq, d = x.shape
    d_inner = wz.shape[1]
    conv_dim = wxbc.shape[1]
    gn = N_GROUPS * D_STATE
    heads = d_inner // HEAD_DIM
    kw = cw.shape[0]
    consts = (gpre, wz, wxbc, wdt, cw, cb, dtb, alog, dexp, ng, wout, gpost)
    return pl.pallas_call(
        _ssd_prompt_kernel,
        grid=(bsz, seq // tile),
        in_specs=[pl.BlockSpec((None, tile, d), lambda b, t: (b, t, 0))]
        + [_const_spec(c.shape) for c in consts],
        out_specs=[
            pl.BlockSpec((None, tile, d), lambda b, t: (b, t, 0)),
            pl.BlockSpec((None, heads, HEAD_DIM, D_STATE), lambda b, t: (b, 0, 0, 0)),
            pl.BlockSpec((None, kw - 1, conv_dim), lambda b, t: (b, 0, 0)),
        ],
        out_shape=[
            jax.ShapeDtypeStruct((bsz, seq, d), F32),
            jax.ShapeDtypeStruct((bsz, heads, HEAD_DIM, D_STATE), F32),
            jax.ShapeDtypeStruct((bsz, kw - 1, conv_dim), F32),
        ],
        scratch_shapes=[
            pltpu.VMEM((N_GROUPS, D_STATE, HEADS_PER_GROUP * HEAD_DIM), F32),
            pltpu.VMEM((8 + tile, conv_dim), F32),
            pltpu.VMEM((tile, d_inner), F32),
            pltpu.VMEM((tile, d_inner), F32),
            pltpu.VMEM((tile, gn), BF16),
            pltpu.VMEM((tile, gn), BF16),
            pltpu.VMEM((tile, d_inner), F32),
            pltpu.VMEM((tile, LANES), F32),
            pltpu.VMEM((tile, d_inner), F32),
        ],
        compiler_params=pltpu.CompilerParams(
            dimension_semantics=("arbitrary", "arbitrary"),
            vmem_limit_bytes=VMEM_LIMIT_BYTES),
        name="ssd_prompt_layer",
    )(x, *consts)


def _cfm_prompt_kernel(xa_ref, xb_ref, gpre_ref, wvz_ref, wg_ref, bvz_ref, bg_ref,
                       cw_ref, cb_ref, lng_ref, lnb_ref, wout_ref, gpost_ref,
                       y_ref, cout_ref, up_s, ext_s, z_s, acc_s, *, tiles_per_seq):
    s = pl.program_id(0)
    n_tiles = pl.num_programs(0) - 1
    tile = xa_ref.shape[0]
    n_cblk, _, cblk = wg_ref.shape
    lanes_per_cblk = cblk // LANES
    kw = cw_ref.shape[1]
    nslab = tile // 8
    nh = kw - 1
    hb = CFM_HALO_SLABS * 8
    nxt = s % 2
    cur = 1 - nxt
    first_of_seq = (jnp.minimum(s, n_tiles - 1) % tiles_per_seq) == 0

    @pl.when(s == 0)
    def _():
        ext_s[...] = jnp.zeros_like(ext_s)
        z_s[...] = jnp.zeros_like(z_s)

    ri = lax.broadcasted_iota(jnp.int32, (tile, tile), 0)
    ci = lax.broadcasted_iota(jnp.int32, (tile, tile), 1)
    to_perm = (ci == (ri & 7) * nslab + (ri >> 3)).astype(BF16)
    to_nat = (ri == (ci & 7) * nslab + (ci >> 3)).astype(BF16)
    u = _rms(xa_ref[...], gpre_ref[...]).astype(BF16)
    up_s[0:tile, :] = _dot(to_perm, u).astype(BF16)
    sub = lax.broadcasted_iota(jnp.int32, (nh * 8, LANES), 0) & 7

    def block_body(c, carry):
        lbs = [c * lanes_per_cblk + e for e in range(lanes_per_cblk)]
        prev_tails = [ext_s[cur, lb, hb + (nslab - nh) * 8:hb + tile, :] for lb in lbs]

        def conv_unit(j0, lb):
            accs = [jnp.broadcast_to(cb_ref[lb], (8, LANES))] * CFM_SLAB_GROUP
            for m in range(j0, j0 + CFM_SLAB_GROUP + kw - 1):
                e_m = ext_s[cur, lb, pl.ds((CFM_HALO_SLABS - nh + m) * 8, 8), :]
                for jj in range(CFM_SLAB_GROUP):
                    k = m - j0 - jj
                    if 0 <= k < kw:
                        accs[jj] = accs[jj] + jnp.broadcast_to(cw_ref[lb, k:k + 1, :], (8, LANES)) * e_m
            for jj in range(CFM_SLAB_GROUP):
                acc_s[lb, pl.ds((j0 + jj) * 8, 8), :] = accs[jj]
            return accs[-1]

        def gated_input(after):
            up_s[tile:tile + 16, 0:LANES] = jnp.concatenate([after, after], axis=0).astype(BF16)
            return up_s[pl.ds(pl.multiple_of(jnp.minimum(s, 0) * 16, 16), tile), :]

        units = [(j0, lb) for j0 in range(0, nslab, CFM_SLAB_GROUP) for lb in lbs]
        sig = _sigmoid(_dot(gated_input(conv_unit(*units[0])), wg_ref[c]) + bg_ref[c])
        val_z = _dot(gated_input(conv_unit(*units[1])), wvz_ref[c]) + bvz_ref[c]
        z_s[nxt, c] = val_z[:, cblk:]
        for unit in units[2:]:
            conv_unit(*unit)
        v = val_z[:, :cblk] * sig
        cout_ref[c] = v.reshape(nslab, 8, cblk)[nslab - nh:, 7, :]
        for e in range(lanes_per_cblk):
            lb = c * lanes_per_cblk + e
            v_e = v[:, e * LANES:(e + 1) * LANES]
            prev_tail = jnp.where(first_of_seq, 0.0, prev_tails[e])
            mixed = jnp.where(sub == 7, prev_tail, v_e[(nslab - nh) * 8:, :]).reshape(nh, 8, LANES)
            halo = jnp.concatenate([mixed[:, 7:8, :], mixed[:, 0:7, :]], axis=1)
            ext_s[nxt, lb, hb - nh * 8:hb, :] = halo.reshape(nh * 8, LANES)
            ext_s[nxt, lb, hb:hb + tile, :] = v_e
        return carry

    lax.fori_loop(0, n_cblk, block_body, 0)

    acc = jnp.concatenate([acc_s[lb] for lb in range(n_cblk * lanes_per_cblk)], axis=1)
    xc = acc - jnp.mean(acc, axis=-1, keepdims=True)
    rstd = lax.rsqrt(jnp.mean(xc * xc, axis=-1, keepdims=True) + EPS)
    o = jnp.zeros(y_ref.shape, F32)
    for c in range(n_cblk):
        cols = slice(c * cblk, (c + 1) * cblk)
        cact = _silu(xc[:, cols] * rstd * lng_ref[:, cols] + lnb_ref[:, cols])
        m_p = (cact * _silu(z_s[cur, c])).astype(BF16)
        m_c = _dot(to_nat, m_p).astype(BF16)
        o = o + _dot(m_c, wout_ref[cols, :])
    y_ref[...] = xb_ref[...] + _rms(o, gpost_ref[...])


def _cfm_prompt_layer(x, gpre, wvz, wg, bvz, bg, cw, cb, lng, lnb, wout, gpost, *, tile):
    bsz, seq, d = x.shape
    n_cblk, _, cblk = wg.shape
    width = n_cblk * cblk
    n_lblk, kw, _ = cw.shape
    tiles_per_seq = seq // tile
    n_tiles = bsz * tiles_per_seq
    assert kw - 1 <= CFM_HALO_SLABS <= tile // 8 and seq % tile == 0 and n_lblk * LANES == width
    consts = (gpre, wvz, wg, bvz, bg, cw, cb, lng, lnb, wout, gpost)

    def tile_a(s):
        ta = jnp.minimum(s, n_tiles - 1)
        return ta // tiles_per_seq, ta % tiles_per_seq

    def tile_b(s):
        tb = jnp.maximum(s - 1, 0)
        return tb // tiles_per_seq, tb % tiles_per_seq

    y, cout = pl.pallas_call(
        functools.partial(_cfm_prompt_kernel, tiles_per_seq=tiles_per_seq),
        grid=(n_tiles + 1,),
        in_specs=[pl.BlockSpec((None, tile, d), lambda s: (*tile_a(s), 0)),
                  pl.BlockSpec((None, tile, d), lambda s: (*tile_b(s), 0))]
        + [_const_spec(c.shape) for c in consts],
        out_specs=[
            pl.BlockSpec((None, tile, d), lambda s: (*tile_b(s), 0)),
            pl.BlockSpec((None, n_cblk, kw - 1, cblk), lambda s: (tile_a(s)[0], 0, 0, 0)),
        ],
        out_shape=[
            jax.ShapeDtypeStruct((bsz, seq, d), F32),
            jax.ShapeDtypeStruct((bsz, n_cblk, kw - 1, cblk), F32),
        ],
        scratch_shapes=[
            pltpu.VMEM((tile + 16, d), BF16),
            pltpu.VMEM((2, n_lblk, CFM_HALO_SLABS * 8 + tile, LANES), F32),
            pltpu.VMEM((2, n_cblk, tile, cblk), F32),
            pltpu.VMEM((n_lblk, tile, LANES), F32),
        ],
        compiler_params=pltpu.CompilerParams(
            dimension_semantics=("arbitrary",),
            vmem_limit_bytes=VMEM_LIMIT_BYTES),
        name="cfm_prompt_layer",
    )(x, x, *consts)
    return y, jnp.transpose(cout, (0, 2, 1, 3)).reshape(bsz, kw - 1, width)


def _single_step_call(kernel, out_shapes, *args, name):
    return pl.pallas_call(
        kernel,
        grid=(1,),
        in_specs=[_const_spec(a.shape) for a in args],
        out_specs=[_const_spec(s.shape) for s in out_shapes],
        out_shape=out_shapes,
        compiler_params=pltpu.CompilerParams(
            dimension_semantics=("arbitrary",),
            vmem_limit_bytes=VMEM_LIMIT_BYTES),
        name=name,
    )(*args)


def _ssd_sample_front_kernel(x_ref, gpre_ref, wz_ref, wxbc_ref, wdt_ref, cst_ref, cw_ref, cb_ref,
                             dtb_ref, alog_ref,
                             z_ref, xs_ref, b_ref, c_ref, xdt_t_ref, da_ref, cst_out_ref):
    d_inner = xs_ref.shape[1]
    gn = N_GROUPS * D_STATE
    kw = cw_ref.shape[0]
    u = _rms(x_ref[...], gpre_ref[...]).astype(BF16)
    z_ref[...] = _dot(u, wz_ref[...])
    xbc = _dot(u, wxbc_ref[...])
    dt_raw = _dot(u, wdt_ref[...])

    acc = cb_ref[...] + cw_ref[kw - 1:kw, :] * xbc
    for k in range(kw - 1):
        acc = acc + cw_ref[k:k + 1, :] * cst_ref[k]
    for k in range(kw - 2):
        cst_out_ref[k] = cst_ref[k + 1]
    cst_out_ref[kw - 2] = xbc
    act = _silu(acc)
    xs = act[:, :d_inner]
    xs_ref[...] = xs
    b_ref[...] = act[:, d_inner:d_inner + gn]
    c_ref[...] = act[:, d_inner + gn:]

    dt = _softplus(dt_raw + dtb_ref[...])
    da_ref[...] = jnp.exp(dt * (-jnp.exp(alog_ref[...])))[:, :da_ref.shape[1]]
    hrow = lax.broadcasted_iota(jnp.int32, (LANES, d_inner), 0)
    ccol = lax.broadcasted_iota(jnp.int32, (LANES, d_inner), 1)
    expand = ((ccol >> 6) == hrow).astype(BF16)
    xdt_t_ref[...] = (xs * _dot_f32_lhs(dt, expand)).T.astype(BF16)


def _ssd_sample_state_kernel(da_ref, h_ref, xdt_t_ref, ball_ref, c_ref, hn_ref, yt_ref):
    step = pl.program_id(0)
    tok_block = h_ref.shape[0]
    ntok = ball_ref.shape[0]
    rpg = HEADS_PER_GROUP * HEAD_DIM
    tok_row = lax.broadcasted_iota(jnp.int32, (ntok, D_STATE), 0)
    tok_lane = lax.broadcasted_iota(jnp.int32, (rpg, ntok), 1)
    blk_row = lax.broadcasted_iota(jnp.int32, (tok_block, D_STATE), 0)

    @pl.when(step == 0)
    def _():
        yt_ref[...] = jnp.zeros_like(yt_ref)

    def token_body(j, carry):
        tok = step * tok_block + j
        for g in range(N_GROUPS):
            gl = slice(g * D_STATE, (g + 1) * D_STATE)
            gr = slice(g * rpg, (g + 1) * rpg)
            b_tok = jnp.where(tok_row == tok, ball_ref[:, gl], 0.0).astype(BF16)
            outer = _dot(xdt_t_ref[gr, :], b_tok)
            scaled = [h_ref[j, (g * HEADS_PER_GROUP + r) * HEAD_DIM:(g * HEADS_PER_GROUP + r + 1) * HEAD_DIM, :]
                      * da_ref[tok, g * HEADS_PER_GROUP + r] for r in range(HEADS_PER_GROUP)]
            hn_g = jnp.concatenate(scaled, axis=0) + outer
            hn_ref[j, gr, :] = hn_g
            c_tok = jnp.sum(jnp.where(blk_row == j, c_ref[:, gl], 0.0), axis=0, keepdims=True)
            ycol = jnp.sum(hn_g * c_tok, axis=1, keepdims=True)
            yt_ref[gr, :] = jnp.where(tok_lane == tok, ycol, yt_ref[gr, :])
        return carry

    lax.fori_loop(0, tok_block, token_body, 0)


def _ssd_sample_state(da, h0, xdt_t, b_all, c_all, *, tok_block):
    ntok, rows, n = h0.shape
    return pl.pallas_call(
        _ssd_sample_state_kernel,
        grid=(ntok // tok_block,),
        in_specs=[
            pl.BlockSpec(memory_space=pltpu.SMEM),
            pl.BlockSpec((tok_block, rows, n), lambda i: (i, 0, 0)),
            _const_spec(xdt_t.shape),
            _const_spec(b_all.shape),
            pl.BlockSpec((tok_block, c_all.shape[1]), lambda i: (i, 0)),
        ],
        out_specs=[
            pl.BlockSpec((tok_block, rows, n), lambda i: (i, 0, 0)),
            pl.BlockSpec((rows, ntok), lambda i: (0, 0)),
        ],
        out_shape=[
            jax.ShapeDtypeStruct(h0.shape, F32),
            jax.ShapeDtypeStruct((rows, ntok), F32),
        ],
        compiler_params=pltpu.CompilerParams(
            dimension_semantics=("arbitrary",),
            vmem_limit_bytes=VMEM_LIMIT_BYTES),
        name="ssd_sample_state",
    )(da, h0, xdt_t, b_all, c_all)


def _ssd_sample_tail_kernel(yt_ref, xs_ref, z_ref, x_ref, dexp_ref, ng_ref, wout_ref, gpost_ref, o_ref):
    y = yt_ref[...].T + dexp_ref[...] * xs_ref[...]
    yg = y * _silu(z_ref[...])
    yn = _rms(yg, ng_ref[...]).astype(BF16)
    o = _dot(yn, wout_ref[...])
    o_ref[...] = x_ref[...] + _rms(o, gpost_ref[...])


def _cfm_sample_front_kernel(x_ref, gpre_ref, wv_ref, wg_ref, wz_ref, bv_ref, bg_ref, bz_ref,
                             v_ref, z_ref):
    u = _rms(x_ref[...], gpre_ref[...]).astype(BF16)
    val = _dot(u, wv_ref[...]) + bv_ref[...]
    gate = _dot(u, wg_ref[...]) + bg_ref[...]
    z_ref[...] = _dot(u, wz_ref[...]) + bz_ref[...]
    v_ref[...] = val * _sigmoid(gate)


def _cfm_sample_conv_kernel(cst_ref, v_ref, cw_ref, cb_ref, c_ref, cst_out_ref):
    kw = cw_ref.shape[0]
    v = v_ref[...]
    acc = cb_ref[...] + cw_ref[kw - 1:kw, :] * v
    for k in range(kw - 1):
        acc = acc + cw_ref[k:k + 1, :] * cst_ref[k]
    c_ref[...] = acc
    for k in range(kw - 2):
        cst_out_ref[k] = cst_ref[k + 1]
    cst_out_ref[kw - 2] = v


def _cfm_sample_conv(cst, v, cw, cb, *, tok_block):
    taps, ntok, width = cst.shape
    return pl.pallas_call(
        _cfm_sample_conv_kernel,
        grid=(ntok // tok_block,),
        in_specs=[
            pl.BlockSpec((taps, tok_block, width), lambda i: (0, i, 0)),
            pl.BlockSpec((tok_block, width), lambda i: (i, 0)),
            _const_spec(cw.shape),
            _const_spec(cb.shape),
        ],
        out_specs=[
            pl.BlockSpec((tok_block, width), lambda i: (i, 0)),
            pl.BlockSpec((taps, tok_block, width), lambda i: (0, i, 0)),
        ],
        out_shape=[
            jax.ShapeDtypeStruct((ntok, width), F32),
            jax.ShapeDtypeStruct(cst.shape, F32),
        ],
        compiler_params=pltpu.CompilerParams(
            dimension_semantics=("arbitrary",),
            vmem_limit_bytes=VMEM_LIMIT_BYTES),
        name="cfm_sample_conv",
    )(cst, v, cw, cb)


def _cfm_sample_tail_kernel(c_ref, z_ref, x_ref, lng_ref, lnb_ref, wout_ref, gpost_ref, o_ref):
    c = _silu(_layer_norm(c_ref[...], lng_ref[...], lnb_ref[...]))
    o = _dot((c * _silu(z_ref[...])).astype(BF16), wout_ref[...])
    o_ref[...] = x_ref[...] + _rms(o, gpost_ref[...])


def _row(v):
    return v.reshape(1, -1)


def _pad_lanes(a):
    pad = (-a.shape[-1]) % LANES
    return jnp.pad(a, [(0, 0)] * (a.ndim - 1) + [(0, pad)])


def kernel(x_prompt, x_sample, state_ssm, state_conv_ssm, state_conv_cfm, g_pre, g_post, ssm_w_in, ssm_conv_w, ssm_conv_b, ssm_dt_bias, ssm_A_log, ssm_D, ssm_norm_g, ssm_w_out, cfm_w_in, cfm_b_in, cfm_conv_w, cfm_conv_b, cfm_ln_g, cfm_ln_b, cfm_w_out):
    d_inner = ssm_w_out.shape[1]
    conv_dim = ssm_conv_w.shape[2]
    heads = ssm_D.shape[1]
    width = cfm_w_out.shape[1]
    ntok = x_sample.shape[0]
    d = x_sample.shape[2]

    w_in = ssm_w_in[0]
    wz = w_in[:, :d_inner].astype(BF16)
    wxbc = w_in[:, d_inner:d_inner + conv_dim].astype(BF16)
    wdt = _pad_lanes(w_in[:, d_inner + conv_dim:]).astype(BF16)
    cw0, cb0 = ssm_conv_w[0], _row(ssm_conv_b[0])
    dtb = _pad_lanes(_row(ssm_dt_bias[0]))
    alog = _pad_lanes(_row(ssm_A_log[0]))
    dexp = _row(jnp.repeat(ssm_D[0], d_inner // heads))
    ng = _row(ssm_norm_g[0])
    wout0 = ssm_w_out[0].astype(BF16)
    gpre0, gpost0 = _row(g_pre[0]), _row(g_post[0])

    w1 = cfm_w_in[0]
    wv, wg, wz1 = (w1[:, i * width:(i + 1) * width].astype(BF16) for i in range(3))
    bv, bg, bz = (_row(cfm_b_in[0][i * width:(i + 1) * width]) for i in range(3))
    cw1, cb1 = cfm_conv_w[0], _row(cfm_conv_b[0])
    lng, lnb = _row(cfm_ln_g[0]), _row(cfm_ln_b[0])
    wout1 = cfm_w_out[0].astype(BF16)
    gpre1, gpost1 = _row(g_pre[1]), _row(g_post[1])

    xp1, ssm_p, cssm_p = _ssd_prompt_layer(
        x_prompt, gpre0, wz, wxbc, wdt, cw0, cb0, dtb, alog, dexp, ng, wout0, gpost0, tile=256)
    def col_blocks(w):
        return jnp.transpose(w.reshape(w.shape[0], width // CFM_CBLK, CFM_CBLK), (1, 0, 2))

    def lane_blocks(a):
        return jnp.transpose(a.reshape(a.shape[0], width // LANES, LANES), (1, 0, 2))

    xp2, ccfm_p = _cfm_prompt_layer(
        xp1, gpre1,
        jnp.concatenate([col_blocks(wv), col_blocks(wz1)], axis=2), col_blocks(wg),
        jnp.concatenate([col_blocks(bv), col_blocks(bz)], axis=2), col_blocks(bg),
        lane_blocks(cw1), lane_blocks(cb1), lng, lnb, wout1, gpost1, tile=256)

    xs0 = x_sample.reshape(ntok, d)
    cst0 = jnp.transpose(state_conv_ssm[0], (1, 0, 2))
    gn = N_GROUPS * D_STATE
    sds = jax.ShapeDtypeStruct
    z_s, xs_s, b_s, c_s, xdt_t, da, cst0_new = _single_step_call(
        _ssd_sample_front_kernel,
        [sds((ntok, d_inner), F32), sds((ntok, d_inner), F32), sds((ntok, gn), F32),
         sds((ntok, gn), F32), sds((d_inner, ntok), BF16), sds((ntok, heads), F32),
         sds(cst0.shape, F32)],
        xs0, gpre0, wz, wxbc, wdt, cst0, cw0, cb0, dtb, alog,
        name="ssd_sample_front")
    h0 = state_ssm[0].reshape(ntok, heads * HEAD_DIM, D_STATE)
    h_new, y_t = _ssd_sample_state(da, h0, xdt_t, b_s, c_s, tok_block=8)
    (xs1,) = _single_step_call(
        _ssd_sample_tail_kernel, [sds((ntok, d), F32)],
        y_t, xs_s, z_s, xs0, dexp, ng, wout0, gpost0, name="ssd_sample_tail")
    v_s, zc_s = _single_step_call(
        _cfm_sample_front_kernel, [sds((ntok, width), F32), sds((ntok, width), F32)],
        xs1, gpre1, wv, wg, wz1, bv, bg, bz, name="cfm_sample_front")
    c_conv, cst1_new = _cfm_sample_conv(
        jnp.transpose(state_conv_cfm[0], (1, 0, 2)), v_s, cw1, cb1, tok_block=16)
    (xs2,) = _single_step_call(
        _cfm_sample_tail_kernel, [sds((ntok, d), F32)],
        c_conv, zc_s, xs1, lng, lnb, wout1, gpost1, name="cfm_sample_tail")

    return (xp2,
            xs2.reshape(x_sample.shape),
            ssm_p[None],
            cssm_p[None],
            ccfm_p[None],
            h_new.reshape(state_ssm.shape),
            jnp.transpose(cst0_new, (1, 0, 2))[None],
            jnp.transpose(cst1_new, (1, 0, 2))[None])
```

```python
import functools

import jax
import jax.numpy as jnp
from jax import lax
from jax.experimental import pallas as pl
from jax.experimental.pallas import tpu as pltpu

F32 = jnp.float32
BF16 = jnp.bfloat16

EPS = 1e-6
LANES = 128
CHUNK = 128
HEAD_DIM = 64
D_STATE = 128
N_GROUPS = 4
HEADS_PER_GROUP = 8
CFM_HALO_SLABS = 32
CFM_CBLK = 256
CFM_SLAB_GROUP = 8
VMEM_LIMIT_BYTES = 56 * 1024 * 1024


def _dot(a, b):
    return jnp.dot(a, b, preferred_element_type=F32)


def _dot_nt(a, b):
    return lax.dot_general(a, b, (((1,), (1,)), ((), ())), preferred_element_type=F32)


def _dot_tn(a, b):
    return lax.dot_general(a, b, (((0,), (0,)), ((), ())), preferred_element_type=F32)


def _rms(x, g):
    return x * lax.rsqrt(jnp.mean(x * x, axis=-1, keepdims=True) + EPS) * g


def _layer_norm(x, g, b):
    mu = jnp.mean(x, axis=-1, keepdims=True)
    xc = x - mu
    return xc * lax.rsqrt(jnp.mean(xc * xc, axis=-1, keepdims=True) + EPS) * g + b


def _sigmoid(x):
    return 1.0 / (1.0 + jnp.exp(-x))


def _silu(x):
    return x * _sigmoid(x)


def _softplus(x):
    return jnp.maximum(x, 0.0) + jnp.log1p(jnp.exp(-jnp.abs(x)))


def _dot_f32_rhs(l_bf16, a):
    hi = a.astype(BF16)
    r1 = a - hi.astype(F32)
    mid = r1.astype(BF16)
    lo = (r1 - mid.astype(F32)).astype(BF16)
    return _dot(l_bf16, hi) + _dot(l_bf16, mid) + _dot(l_bf16, lo)


def _dot_f32_lhs(a, r_bf16):
    hi = a.astype(BF16)
    r1 = a - hi.astype(F32)
    mid = r1.astype(BF16)
    lo = (r1 - mid.astype(F32)).astype(BF16)
    return _dot(hi, r_bf16) + _dot(mid, r_bf16) + _dot(lo, r_bf16)


def _col(a, j):
    return a[:, j:j + 1]


def _ssd_prompt_kernel(x_ref, gpre_ref, wz_ref, wxbc_ref, wdt_ref, cw_ref, cb_ref,
                       dtb_ref, alog_ref, dexp_ref, ng_ref, wout_ref, gpost_ref,
                       y_ref, hout_ref, cout_ref,
                       ht_s, tail_s, xs_s, xdt_s, b_s, c_s, z_s, a_s, yb_s):
    t = pl.program_id(1)
    nt = pl.num_programs(1)
    tile = x_ref.shape[0]
    n_chunks = tile // CHUNK
    d_inner = xs_s.shape[1]
    gn = N_GROUPS * D_STATE
    kw = cw_ref.shape[0]
    halo = 8

    @pl.when(t == 0)
    def _():
        ht_s[...] = jnp.zeros_like(ht_s)
        tail_s[...] = jnp.zeros_like(tail_s)

    x = x_ref[...]
    u = _rms(x, gpre_ref[...]).astype(BF16)
    z_s[...] = _dot(u, wz_ref[...])
    xbc = _dot(u, wxbc_ref[...])
    dt_raw = _dot(u, wdt_ref[...])

    conv_dim = xbc.shape[1]
    rows8 = jnp.concatenate([tail_s[...].reshape(1, halo, conv_dim),
                             xbc.reshape(tile // halo, halo, conv_dim)], axis=0)
    sub = lax.broadcasted_iota(jnp.int32, (tile // halo, halo, conv_dim), 1)
    acc = cb_ref[...] + cw_ref[kw - 1:kw, :] * xbc
    for sh in range(1, kw):
        rot = jnp.concatenate([rows8[:, halo - sh:, :], rows8[:, :halo - sh, :]], axis=1)
        win = jnp.where(sub < sh, rot[:-1], rot[1:]).reshape(tile, conv_dim)
        acc = acc + cw_ref[kw - 1 - sh:kw - sh, :] * win
    last8 = xbc[tile - halo:, :]
    cout_ref[...] = last8[halo - (kw - 1):, :]
    tail_s[...] = last8
    act = _silu(acc)
    xs = act[:, :d_inner]
    xs_s[...] = xs
    b_s[...] = act[:, d_inner:d_inner + gn].astype(BF16)
    c_s[...] = act[:, d_inner + gn:].astype(BF16)
    dt = _softplus(dt_raw + dtb_ref[...])
    a_s[...] = dt * (-jnp.exp(alog_ref[...]))
    hrow = lax.broadcasted_iota(jnp.int32, (LANES, d_inner), 0)
    ccol = lax.broadcasted_iota(jnp.int32, (LANES, d_inner), 1)
    expand = ((ccol >> 6) == hrow).astype(BF16)
    xdt_s[...] = xs * _dot_f32_lhs(dt, expand)

    row = lax.broadcasted_iota(jnp.int32, (CHUNK, CHUNK), 0)
    lane = lax.broadcasted_iota(jnp.int32, (CHUNK, CHUNK), 1)
    causal = row >= lane
    tri = causal.astype(BF16)
    lo_half = lane < HEAD_DIM
    lo_half_row = lo_half[0:1, :]
    dexp = dexp_ref[...]

    def chunk_body(c, carry):
        r0 = pl.multiple_of(c * CHUNK, CHUNK)
        rows = pl.ds(r0, CHUNK)
        a_c = a_s[rows, :]
        a_cum = _dot_f32_rhs(tri, a_c)
        a_cum_t = a_cum.T
        a_last = a_cum[CHUNK - 1:CHUNK, :]
        for g in range(N_GROUPS):
            gl = slice(g * D_STATE, (g + 1) * D_STATE)
            cg = c_s[rows, gl]
            bg = b_s[rows, gl]
            cb = _dot_nt(cg, bg)
            ht_g = ht_s[g]
            yoff_g = _dot(cg, ht_g.astype(BF16))
            xw_parts = []
            dec_parts = []
            for j in range(HEADS_PER_GROUP // 2):
                h0 = g * HEADS_PER_GROUP + 2 * j
                h1 = h0 + 1
                pl_ = slice((g * 4 + j) * LANES, (g * 4 + j + 1) * LANES)
                xdt_p = xdt_s[rows, pl_]
                acol0 = jnp.broadcast_to(_col(a_cum, h0), (CHUNK, CHUNK))
                acol1 = jnp.broadcast_to(_col(a_cum, h1), (CHUNK, CHUNK))

                def scores(h, acol):
                    diff = acol - a_cum_t[h:h + 1, :]
                    dec = jnp.exp(jnp.where(causal, diff, -jnp.inf))
                    return (cb * dec).astype(BF16)

                x0 = jnp.where(lo_half, xdt_p, 0.0).astype(BF16)
                x1 = jnp.where(lo_half, 0.0, xdt_p).astype(BF16)
                y_diag = _dot(scores(h0, acol0), x0) + _dot(scores(h1, acol1), x1)
                a_pair = jnp.where(lo_half, acol0, acol1)
                last_pair = jnp.where(lo_half_row, a_last[:, h0:h0 + 1], a_last[:, h1:h1 + 1])
                y_pair = (y_diag + jnp.exp(a_pair) * yoff_g[:, j * LANES:(j + 1) * LANES]
                          + dexp[:, pl_] * xs_s[rows, pl_])
                yb_s[rows, pl_] = y_pair
                xw_parts.append((xdt_p * jnp.exp(last_pair - a_pair)).astype(BF16))
                dec_parts.append(jnp.exp(last_pair))
            xw_g = jnp.concatenate(xw_parts, axis=1)
            dec_g = jnp.concatenate(dec_parts, axis=1)
            ht_s[g] = ht_g * dec_g + _dot_tn(bg, xw_g)
        return carry

    lax.fori_loop(0, n_chunks, chunk_body, 0)

    yg = yb_s[...] * _silu(z_s[...])
    yn = _rms(yg, ng_ref[...]).astype(BF16)
    o = _dot(yn, wout_ref[...])
    y_ref[...] = x + _rms(o, gpost_ref[...])

    @pl.when(t == nt - 1)
    def _():
        for g in range(N_GROUPS):
            hg = ht_s[g].T
            for r in range(HEADS_PER_GROUP):
                hout_ref[g * HEADS_PER_GROUP + r] = hg[r * HEAD_DIM:(r + 1) * HEAD_DIM, :]


def _const_spec(shape):
    nd = len(shape)
    return pl.BlockSpec(shape, lambda *_: (0,) * nd, pipeline_mode=pl.Buffered(1))


def _ssd_prompt_layer(x, gpre, wz, wxbc, wdt, cw, cb, dtb, alog, dexp, ng, wout, gpost, *, tile):
    bsz, seq, d = x.shape
    d_inner = wz.shape[1]
    conv_dim = wxbc.shape[1]
    gn = N_GROUPS * D_STATE
    heads = d_inner // HEAD_DIM
    kw = cw.shape[0]
    consts = (gpre, wz, wxbc, wdt, cw, cb, dtb, alog, dexp, ng, wout, gpost)
    return pl.pallas_call(
        _ssd_prompt_kernel,
        grid=(bsz, seq // tile),
        in_specs=[pl.BlockSpec((None, tile, d), lambda b, t: (b, t, 0))]
        + [_const_spec(c.shape) for c in consts],
        out_specs=[
            pl.BlockSpec((None, tile, d), lambda b, t: (b, t, 0)),
            pl.BlockSpec((None, heads, HEAD_DIM, D_STATE), lambda b, t: (b, 0, 0, 0)),
            pl.BlockSpec((None, kw - 1, conv_dim), lambda b, t: (b, 0, 0)),
        ],
        out_shape=[
            jax.ShapeDtypeStruct((bsz, seq, d), F32),
            jax.ShapeDtypeStruct((bsz, heads, HEAD_DIM, D_STATE), F32),
            jax.ShapeDtypeStruct((bsz, kw - 1, conv_dim), F32),
        ],
        scratch_shapes=[
            pltpu.VMEM((N_GROUPS, D_STATE, HEADS_PER_GROUP * HEAD_DIM), F32),
            pltpu.VMEM((8, conv_dim), F32),
            pltpu.VMEM((tile, d_inner), F32),
            pltpu.VMEM((tile, d_inner), F32),
            pltpu.VMEM((tile, gn), BF16),
            pltpu.VMEM((tile, gn), BF16),
            pltpu.VMEM((tile, d_inner), F32),
            pltpu.VMEM((tile, LANES), F32),
            pltpu.VMEM((tile, d_inner), F32),
        ],
        compiler_params=pltpu.CompilerParams(
            dimension_semantics=("arbitrary", "arbitrary"),
            vmem_limit_bytes=VMEM_LIMIT_BYTES),
        name="ssd_prompt_layer",
    )(x, *consts)


def _cfm_prompt_kernel(xa_ref, xb_ref, gpre_ref, win_ref, bin_ref,
                       cw_ref, cb_ref, lng_ref, lnb_ref, wout_ref, gpost_ref,
                       y_ref, cout_ref, up_s, ext_s, z_s, acc_s, taps_s, *, tiles_per_seq):
    s = pl.program_id(0)
    n_tiles = pl.num_programs(0) - 1
    tile = xa_ref.shape[0]
    n_cblk = win_ref.shape[0]
    cblk = win_ref.shape[2] // 3
    lanes_per_cblk = cblk // LANES
    kw = cw_ref.shape[1]
    nslab = tile // 8
    nh = kw - 1
    hb = CFM_HALO_SLABS * 8
    nxt = s % 2
    cur = 1 - nxt
    first_of_seq = (jnp.minimum(s, n_tiles - 1) % tiles_per_seq) == 0

    @pl.when(s == 0)
    def _():
        ext_s[...] = jnp.zeros_like(ext_s)
        z_s[...] = jnp.zeros_like(z_s)
        for lb in range(n_cblk * lanes_per_cblk):
            for k in range(kw):
                taps_s[lb, k * 8:(k + 1) * 8, :] = jnp.broadcast_to(cw_ref[lb, k:k + 1, :], (8, LANES))

    ri = lax.broadcasted_iota(jnp.int32, (tile, tile), 0)
    ci = lax.broadcasted_iota(jnp.int32, (tile, tile), 1)
    to_perm = (ci == (ri & 7) * nslab + (ri >> 3)).astype(BF16)
    to_nat = (ri == (ci & 7) * nslab + (ci >> 3)).astype(BF16)
    u = _rms(xa_ref[...], gpre_ref[...]).astype(BF16)
    up_s[0:tile, :] = _dot(to_perm, u).astype(BF16)
    sub = lax.broadcasted_iota(jnp.int32, (nh * 8, LANES), 0) & 7

    def block_body(c, carry):
        lbs = [c * lanes_per_cblk + e for e in range(lanes_per_cblk)]
        prev_tails = [ext_s[cur, lb, hb + (nslab - nh) * 8:hb + tile, :] for lb in lbs]

        def conv_unit(j0, lb):
            accs = [jnp.broadcast_to(cb_ref[lb], (8, LANES))] * CFM_SLAB_GROUP
            for m in range(j0, j0 + CFM_SLAB_GROUP + kw - 1):
                e_m = ext_s[cur, lb, pl.ds((CFM_HALO_SLABS - nh + m) * 8, 8), :]
                for jj in range(CFM_SLAB_GROUP):
                    k = m - j0 - jj
                    if 0 <= k < kw:
                        accs[jj] = accs[jj] + taps_s[lb, pl.ds(k * 8, 8), :] * e_m
            for jj in range(CFM_SLAB_GROUP):
                acc_s[lb, pl.ds((j0 + jj) * 8, 8), :] = accs[jj]
            return accs[-1]

        def gated_input(after):
            up_s[tile:tile + 16, 0:LANES] = jnp.concatenate([after, after], axis=0).astype(BF16)
            return up_s[pl.ds(pl.multiple_of(jnp.minimum(s, 0) * 16, 16), tile), :]

        units = [(j0, lb) for j0 in range(0, nslab, CFM_SLAB_GROUP) for lb in lbs]
        proj = _dot(gated_input(conv_unit(*units[0])), win_ref[c]) + bin_ref[c]
        part = [proj[:, i * LANES:(i + 1) * LANES] for i in range(6)]
        z_s[nxt, c] = jnp.concatenate([part[3], part[5]], axis=1)
        for unit in units[1:]:
            conv_unit(*unit)
        v = jnp.concatenate([part[1] * _sigmoid(part[0]), part[4] * _sigmoid(part[2])], axis=1)
        cout_ref[c] = v.reshape(nslab, 8, cblk)[nslab - nh:, 7, :]
        for e in range(lanes_per_cblk):
            lb = c * lanes_per_cblk + e
            v_e = v[:, e * LANES:(e + 1) * LANES]
            prev_tail = jnp.where(first_of_seq, 0.0, prev_tails[e])
            mixed = jnp.where(sub == 7, prev_tail, v_e[(nslab - nh) * 8:, :]).reshape(nh, 8, LANES)
            halo = jnp.concatenate([mixed[:, 7:8, :], mixed[:, 0:7, :]], axis=1)
            ext_s[nxt, lb, hb - nh * 8:hb, :] = halo.reshape(nh * 8, LANES)
            ext_s[nxt, lb, hb:hb + tile, :] = v_e
        return carry

    lax.fori_loop(0, n_cblk, block_body, 0)

    acc = jnp.concatenate([acc_s[lb] for lb in range(n_cblk * lanes_per_cblk)], axis=1)
    xc = acc - jnp.mean(acc, axis=-1, keepdims=True)
    rstd = lax.rsqrt(jnp.mean(xc * xc, axis=-1, keepdims=True) + EPS)
    o = jnp.zeros(y_ref.shape, F32)
    for c in range(n_cblk):
        cols = slice(c * cblk, (c + 1) * cblk)
        cact = _silu(xc[:, cols] * rstd * lng_ref[:, cols] + lnb_ref[:, cols])
        m_p = (cact * _silu(z_s[cur, c])).astype(BF16)
        m_c = _dot(to_nat, m_p).astype(BF16)
        o = o + _dot(m_c, wout_ref[cols, :])
    y_ref[...] = xb_ref[...] + _rms(o, gpost_ref[...])


def _cfm_prompt_layer(x, gpre, win, bin_, cw, cb, lng, lnb, wout, gpost, *, tile):
    bsz, seq, d = x.shape
    n_cblk = win.shape[0]
    cblk = win.shape[2] // 3
    width = n_cblk * cblk
    n_lblk, kw, _ = cw.shape
    tiles_per_seq = seq // tile
    n_tiles = bsz * tiles_per_seq
    assert kw - 1 <= CFM_HALO_SLABS <= tile // 8 and seq % tile == 0 and n_lblk * LANES == width
    assert cblk == 2 * LANES
    consts = (gpre, win, bin_, cw, cb, lng, lnb, wout, gpost)

    def tile_a(s):
        ta = jnp.minimum(s, n_tiles - 1)
        return ta // tiles_per_seq, ta % tiles_per_seq

    def tile_b(s):
        tb = jnp.maximum(s - 1, 0)
        return tb // tiles_per_seq, tb % tiles_per_seq

    y, cout = pl.pallas_call(
        functools.partial(_cfm_prompt_kernel, tiles_per_seq=tiles_per_seq),
        grid=(n_tiles + 1,),
        in_specs=[pl.BlockSpec((None, tile, d), lambda s: (*tile_a(s), 0)),
                  pl.BlockSpec((None, tile, d), lambda s: (*tile_b(s), 0))]
        + [_const_spec(c.shape) for c in consts],
        out_specs=[
            pl.BlockSpec((None, tile, d), lambda s: (*tile_b(s), 0)),
            pl.BlockSpec((None, n_cblk, kw - 1, cblk), lambda s: (tile_a(s)[0], 0, 0, 0)),
        ],
        out_shape=[
            jax.ShapeDtypeStruct((bsz, seq, d), F32),
            jax.ShapeDtypeStruct((bsz, n_cblk, kw - 1, cblk), F32),
        ],
        scratch_shapes=[
            pltpu.VMEM((tile + 16, d), BF16),
            pltpu.VMEM((2, n_lblk, CFM_HALO_SLABS * 8 + tile, LANES), F32),
            pltpu.VMEM((2, n_cblk, tile, cblk), F32),
            pltpu.VMEM((n_lblk, tile, LANES), F32),
            pltpu.VMEM((n_lblk, kw * 8, LANES), F32),
        ],
        compiler_params=pltpu.CompilerParams(
            dimension_semantics=("arbitrary",),
            vmem_limit_bytes=VMEM_LIMIT_BYTES),
        name="cfm_prompt_layer",
    )(x, x, *consts)
    return y, jnp.transpose(cout, (0, 2, 1, 3)).reshape(bsz, kw - 1, width)


def _single_step_call(kernel, out_shapes, *args, name):
    return pl.pallas_call(
        kernel,
        grid=(1,),
        in_specs=[_const_spec(a.shape) for a in args],
        out_specs=[_const_spec(s.shape) for s in out_shapes],
        out_shape=out_shapes,
        compiler_params=pltpu.CompilerParams(
            dimension_semantics=("arbitrary",),
            vmem_limit_bytes=VMEM_LIMIT_BYTES),
        name=name,
    )(*args)


def _ssd_sample_front_kernel(x_ref, gpre_ref, wz_ref, wxbc_ref, wdt_ref, cst_ref, cw_ref, cb_ref,
                             dtb_ref, alog_ref,
                             z_ref, xs_ref, b_ref, c_ref, xdt_t_ref, da_ref, cst_out_ref):
    d_inner = xs_ref.shape[1]
    gn = N_GROUPS * D_STATE
    kw = cw_ref.shape[0]
    u = _rms(x_ref[...], gpre_ref[...]).astype(BF16)
    z_ref[...] = _dot(u, wz_ref[...])
    xbc = _dot(u, wxbc_ref[...])
    dt_raw = _dot(u, wdt_ref[...])

    acc = cb_ref[...] + cw_ref[kw - 1:kw, :] * xbc
    for k in range(kw - 1):
        acc = acc + cw_ref[k:k + 1, :] * cst_ref[k]
    for k in range(kw - 2):
        cst_out_ref[k] = cst_ref[k + 1]
    cst_out_ref[kw - 2] = xbc
    act = _silu(acc)
    xs = act[:, :d_inner]
    xs_ref[...] = xs
    b_ref[...] = act[:, d_inner:d_inner + gn]
    c_ref[...] = act[:, d_inner + gn:]

    dt = _softplus(dt_raw + dtb_ref[...])
    da_ref[...] = jnp.exp(dt * (-jnp.exp(alog_ref[...])))[:, :da_ref.shape[1]]
    hrow = lax.broadcasted_iota(jnp.int32, (LANES, d_inner), 0)
    ccol = lax.broadcasted_iota(jnp.int32, (LANES, d_inner), 1)
    expand = ((ccol >> 6) == hrow).astype(BF16)
    xdt_t_ref[...] = (xs * _dot_f32_lhs(dt, expand)).T.astype(BF16)


def _ssd_sample_state_kernel(da_ref, h_ref, xdt_t_ref, ball_ref, c_ref, hn_ref, yt_ref):
    step = pl.program_id(0)
    tok_block = h_ref.shape[0]
    ntok = ball_ref.shape[0]
    rpg = HEADS_PER_GROUP * HEAD_DIM
    tok_row = lax.broadcasted_iota(jnp.int32, (ntok, D_STATE), 0)
    tok_lane = lax.broadcasted_iota(jnp.int32, (rpg, ntok), 1)
    blk_row = lax.broadcasted_iota(jnp.int32, (tok_block, D_STATE), 0)

    @pl.when(step == 0)
    def _():
        yt_ref[...] = jnp.zeros_like(yt_ref)

    def token_body(j, carry):
        tok = step * tok_block + j
        for g in range(N_GROUPS):
            gl = slice(g * D_STATE, (g + 1) * D_STATE)
            gr = slice(g * rpg, (g + 1) * rpg)
            b_tok = jnp.where(tok_row == tok, ball_ref[:, gl], 0.0).astype(BF16)
            outer = _dot(xdt_t_ref[gr, :], b_tok)
            scaled = [h_ref[j, (g * HEADS_PER_GROUP + r) * HEAD_DIM:(g * HEADS_PER_GROUP + r + 1) * HEAD_DIM, :]
                      * da_ref[tok, g * HEADS_PER_GROUP + r] for r in range(HEADS_PER_GROUP)]
            hn_g = jnp.concatenate(scaled, axis=0) + outer
            hn_ref[j, gr, :] = hn_g
            c_tok = jnp.sum(jnp.where(blk_row == j, c_ref[:, gl], 0.0), axis=0, keepdims=True)
            ycol = jnp.sum(hn_g * c_tok, axis=1, keepdims=True)
            yt_ref[gr, :] = jnp.where(tok_lane == tok, ycol, yt_ref[gr, :])
        return carry

    lax.fori_loop(0, tok_block, token_body, 0)


def _ssd_sample_state(da, h0, xdt_t, b_all, c_all, *, tok_block):
    ntok, rows, n = h0.shape
    return pl.pallas_call(
        _ssd_sample_state_kernel,
        grid=(ntok // tok_block,),
        in_specs=[
            pl.BlockSpec(memory_space=pltpu.SMEM),
            pl.BlockSpec((tok_block, rows, n), lambda i: (i, 0, 0)),
            _const_spec(xdt_t.shape),
            _const_spec(b_all.shape),
            pl.BlockSpec((tok_block, c_all.shape[1]), lambda i: (i, 0)),
        ],
        out_specs=[
            pl.BlockSpec((tok_block, rows, n), lambda i: (i, 0, 0)),
            pl.BlockSpec((rows, ntok), lambda i: (0, 0)),
        ],
        out_shape=[
            jax.ShapeDtypeStruct(h0.shape, F32),
            jax.ShapeDtypeStruct((rows, ntok), F32),
        ],
        compiler_params=pltpu.CompilerParams(
            dimension_semantics=("arbitrary",),
            vmem_limit_bytes=VMEM_LIMIT_BYTES),
        name="ssd_sample_state",
    )(da, h0, xdt_t, b_all, c_all)


def _ssd_sample_tail_kernel(yt_ref, xs_ref, z_ref, x_ref, dexp_ref, ng_ref, wout_ref, gpost_ref, o_ref):
    y = yt_ref[...].T + dexp_ref[...] * xs_ref[...]
    yg = y * _silu(z_ref[...])
    yn = _rms(yg, ng_ref[...]).astype(BF16)
    o = _dot(yn, wout_ref[...])
    o_ref[...] = x_ref[...] + _rms(o, gpost_ref[...])


def _cfm_sample_front_kernel(x_ref, gpre_ref, wv_ref, wg_ref, wz_ref, bv_ref, bg_ref, bz_ref,
                             v_ref, z_ref):
    u = _rms(x_ref[...], gpre_ref[...]).astype(BF16)
    val = _dot(u, wv_ref[...]) + bv_ref[...]
    gate = _dot(u, wg_ref[...]) + bg_ref[...]
    z_ref[...] = _dot(u, wz_ref[...]) + bz_ref[...]
    v_ref[...] = val * _sigmoid(gate)


def _cfm_sample_conv_kernel(cst_ref, v_ref, cw_ref, cb_ref, c_ref, cst_out_ref):
    kw = cw_ref.shape[0]
    v = v_ref[...]
    acc = cb_ref[...] + cw_ref[kw - 1:kw, :] * v
    for k in range(kw - 1):
        acc = acc + cw_ref[k:k + 1, :] * cst_ref[k]
    c_ref[...] = acc
    for k in range(kw - 2):
        cst_out_ref[k] = cst_ref[k + 1]
    cst_out_ref[kw - 2] = v


def _cfm_sample_conv(cst, v, cw, cb, *, tok_block):
    taps, ntok, width = cst.shape
    return pl.pallas_call(
        _cfm_sample_conv_kernel,
        grid=(ntok // tok_block,),
        in_specs=[
            pl.BlockSpec((taps, tok_block, width), lambda i: (0, i, 0)),
            pl.BlockSpec((tok_block, width), lambda i: (i, 0)),
            _const_spec(cw.shape),
            _const_spec(cb.shape),
        ],
        out_specs=[
            pl.BlockSpec((tok_block, width), lambda i: (i, 0)),
            pl.BlockSpec((taps, tok_block, width), lambda i: (0, i, 0)),
        ],
        out_shape=[
            jax.ShapeDtypeStruct((ntok, width), F32),
            jax.ShapeDtypeStruct(cst.shape, F32),
        ],
        compiler_params=pltpu.CompilerParams(
            dimension_semantics=("arbitrary",),
            vmem_limit_bytes=VMEM_LIMIT_BYTES),
        name="cfm_sample_conv",
    )(cst, v, cw, cb)


def _cfm_sample_tail_kernel(c_ref, z_ref, x_ref, lng_ref, lnb_ref, wout_ref, gpost_ref, o_ref):
    c = _silu(_layer_norm(c_ref[...], lng_ref[...], lnb_ref[...]))
    o = _dot((c * _silu(z_ref[...])).astype(BF16), wout_ref[...])
    o_ref[...] = x_ref[...] + _rms(o, gpost_ref[...])


def _row(v):
    return v.reshape(1, -1)


def _pad_lanes(a):
    pad = (-a.shape[-1]) % LANES
    return jnp.pad(a, [(0, 0)] * (a.ndim - 1) + [(0, pad)])


def kernel(x_prompt, x_sample, state_ssm, state_conv_ssm, state_conv_cfm, g_pre, g_post, ssm_w_in, ssm_conv_w, ssm_conv_b, ssm_dt_bias, ssm_A_log, ssm_D, ssm_norm_g, ssm_w_out, cfm_w_in, cfm_b_in, cfm_conv_w, cfm_conv_b, cfm_ln_g, cfm_ln_b, cfm_w_out):
    d_inner = ssm_w_out.shape[1]
    conv_dim = ssm_conv_w.shape[2]
    heads = ssm_D.shape[1]
    width = cfm_w_out.shape[1]
    ntok = x_sample.shape[0]
    d = x_sample.shape[2]

    w_in = ssm_w_in[0]
    wz = w_in[:, :d_inner].astype(BF16)
    wxbc = w_in[:, d_inner:d_inner + conv_dim].astype(BF16)
    wdt = _pad_lanes(w_in[:, d_inner + conv_dim:]).astype(BF16)
    cw0, cb0 = ssm_conv_w[0], _row(ssm_conv_b[0])
    dtb = _pad_lanes(_row(ssm_dt_bias[0]))
    alog = _pad_lanes(_row(ssm_A_log[0]))
    dexp = _row(jnp.repeat(ssm_D[0], d_inner // heads))
    ng = _row(ssm_norm_g[0])
    wout0 = ssm_w_out[0].astype(BF16)
    gpre0, gpost0 = _row(g_pre[0]), _row(g_post[0])

    w1 = cfm_w_in[0]
    wv, wg, wz1 = (w1[:, i * width:(i + 1) * width].astype(BF16) for i in range(3))
    bv, bg, bz = (_row(cfm_b_in[0][i * width:(i + 1) * width]) for i in range(3))
    cw1, cb1 = cfm_conv_w[0], _row(cfm_conv_b[0])
    lng, lnb = _row(cfm_ln_g[0]), _row(cfm_ln_b[0])
    wout1 = cfm_w_out[0].astype(BF16)
    gpre1, gpost1 = _row(g_pre[1]), _row(g_post[1])

    xp1, ssm_p, cssm_p = _ssd_prompt_layer(
        x_prompt, gpre0, wz, wxbc, wdt, cw0, cb0, dtb, alog, dexp, ng, wout0, gpost0, tile=256)
    def col_blocks(w):
        return jnp.transpose(w.reshape(w.shape[0], width // CFM_CBLK, CFM_CBLK), (1, 0, 2))

    def mixed_blocks(g, v, z):
        g, v, z = col_blocks(g), col_blocks(v), col_blocks(z)
        lo, hi = slice(0, LANES), slice(LANES, 2 * LANES)
        return jnp.concatenate([g[..., lo], v[..., lo], g[..., hi], z[..., lo], v[..., hi], z[..., hi]], axis=2)

    def lane_blocks(a):
        return jnp.transpose(a.reshape(a.shape[0], width // LANES, LANES), (1, 0, 2))

    xp2, ccfm_p = _cfm_prompt_layer(
        xp1, gpre1,
        mixed_blocks(wg, wv, wz1), mixed_blocks(bg, bv, bz),
        lane_blocks(cw1), lane_blocks(cb1), lng, lnb, wout1, gpost1, tile=256)

    xs0 = x_sample.reshape(ntok, d)
    cst0 = jnp.transpose(state_conv_ssm[0], (1, 0, 2))
    gn = N_GROUPS * D_STATE
    sds = jax.ShapeDtypeStruct
    z_s, xs_s, b_s, c_s, xdt_t, da, cst0_new = _single_step_call(
        _ssd_sample_front_kernel,
        [sds((ntok, d_inner), F32), sds((ntok, d_inner), F32), sds((ntok, gn), F32),
         sds((ntok, gn), F32), sds((d_inner, ntok), BF16), sds((ntok, heads), F32),
         sds(cst0.shape, F32)],
        xs0, gpre0, wz, wxbc, wdt, cst0, cw0, cb0, dtb, alog,
        name="ssd_sample_front")
    h0 = state_ssm[0].reshape(ntok, heads * HEAD_DIM, D_STATE)
    h_new, y_t = _ssd_sample_state(da, h0, xdt_t, b_s, c_s, tok_block=8)
    (xs1,) = _single_step_call(
        _ssd_sample_tail_kernel, [sds((ntok, d), F32)],
        y_t, xs_s, z_s, xs0, dexp, ng, wout0, gpost0, name="ssd_sample_tail")
    v_s, zc_s = _single_step_call(
        _cfm_sample_front_kernel, [sds((ntok, width), F32), sds((ntok, width), F32)],
        xs1, gpre1, wv, wg, wz1, bv, bg, bz, name="cfm_sample_front")
    c_conv, cst1_new = _cfm_sample_conv(
        jnp.transpose(state_conv_cfm[0], (1, 0, 2)), v_s, cw1, cb1, tok_block=16)
    (xs2,) = _single_step_call(
        _cfm_sample_tail_kernel, [sds((ntok, d), F32)],
        c_conv, zc_s, xs1, lng, lnb, wout1, gpost1, name="cfm_sample_tail")

    return (xp2,
            xs2.reshape(x_sample.shape),
            ssm_p[None],
            cssm_p[None],
            ccfm_p[None],
            h_new.reshape(state_ssm.shape),
            jnp.transpose(cst0_new, (1, 0, 2))[None],
            jnp.transpose(cst1_new, (1, 0, 2))[None])
```

```python
import functools

import jax
import jax.numpy as jnp
from jax import lax
from jax.experimental import pallas as pl
from jax.experimental.pallas import tpu as pltpu

F32 = jnp.float32
BF16 = jnp.bfloat16

EPS = 1e-6
LANES = 128
CHUNK = 128
HEAD_DIM = 64
D_STATE = 128
N_GROUPS = 4
HEADS_PER_GROUP = 8
CFM_HALO_SLABS = 32
CFM_CBLK = 256
CFM_SLAB_GROUP = 8
VMEM_LIMIT_BYTES = 56 * 1024 * 1024


def _dot(a, b):
    return jnp.dot(a, b, preferred_element_type=F32)


def _dot_nt(a, b):
    return lax.dot_general(a, b, (((1,), (1,)), ((), ())), preferred_element_type=F32)


def _dot_tn(a, b):
    return lax.dot_general(a, b, (((0,), (0,)), ((), ())), preferred_element_type=F32)


def _rms(x, g):
    return x * lax.rsqrt(jnp.mean(x * x, axis=-1, keepdims=True) + EPS) * g


def _layer_norm(x, g, b):
    mu = jnp.mean(x, axis=-1, keepdims=True)
    xc = x - mu
    return xc * lax.rsqrt(jnp.mean(xc * xc, axis=-1, keepdims=True) + EPS) * g + b


def _sigmoid(x):
    return 1.0 / (1.0 + jnp.exp(-x))


def _silu(x):
    return x * _sigmoid(x)


def _softplus(x):
    return jnp.maximum(x, 0.0) + jnp.log1p(jnp.exp(-jnp.abs(x)))


def _dot_f32_rhs(l_bf16, a):
    hi = a.astype(BF16)
    r1 = a - hi.astype(F32)
    mid = r1.astype(BF16)
    lo = (r1 - mid.astype(F32)).astype(BF16)
    return _dot(l_bf16, hi) + _dot(l_bf16, mid) + _dot(l_bf16, lo)


def _dot_f32_lhs(a, r_bf16):
    hi = a.astype(BF16)
    r1 = a - hi.astype(F32)
    mid = r1.astype(BF16)
    lo = (r1 - mid.astype(F32)).astype(BF16)
    return _dot(hi, r_bf16) + _dot(mid, r_bf16) + _dot(lo, r_bf16)


def _col(a, j):
    return a[:, j:j + 1]


def _ssd_prompt_kernel(x_ref, gpre_ref, win_ref, wdt_ref, cw_ref, cb_ref,
                       dtb_ref, alog_ref, dexp_ref, ng_ref, wout_ref, gpost_ref,
                       y_ref, hout_ref, cout_ref,
                       ht_s, tail_s, xs_s, xdt_s, b_s, c_s, z_s, a_s, yb_s):
    t = pl.program_id(1)
    nt = pl.num_programs(1)
    tile = x_ref.shape[0]
    n_chunks = tile // CHUNK
    d_inner = xs_s.shape[1]
    gn = N_GROUPS * D_STATE
    kw = cw_ref.shape[0]
    halo = 8

    @pl.when(t == 0)
    def _():
        ht_s[...] = jnp.zeros_like(ht_s)
        tail_s[...] = jnp.zeros_like(tail_s)

    x = x_ref[...]
    u = _rms(x, gpre_ref[...]).astype(BF16)
    z_s[...] = _dot(u, win_ref[:, 0:d_inner])
    xbc = _dot(u, win_ref[:, d_inner:d_inner + cw_ref.shape[1]])
    dt_raw = _dot(u, wdt_ref[...])

    conv_dim = xbc.shape[1]
    rows8 = jnp.concatenate([tail_s[...].reshape(1, halo, conv_dim),
                             xbc.reshape(tile // halo, halo, conv_dim)], axis=0)
    sub = lax.broadcasted_iota(jnp.int32, (tile // halo, halo, conv_dim), 1)
    acc = cb_ref[...] + cw_ref[kw - 1:kw, :] * xbc
    for sh in range(1, kw):
        rot = jnp.concatenate([rows8[:, halo - sh:, :], rows8[:, :halo - sh, :]], axis=1)
        win = jnp.where(sub < sh, rot[:-1], rot[1:]).reshape(tile, conv_dim)
        acc = acc + cw_ref[kw - 1 - sh:kw - sh, :] * win
    last8 = xbc[tile - halo:, :]
    cout_ref[...] = last8[halo - (kw - 1):, :]
    tail_s[...] = last8
    act = _silu(acc)
    xs = act[:, :d_inner]
    xs_s[...] = xs
    b_s[...] = act[:, d_inner:d_inner + gn].astype(BF16)
    c_s[...] = act[:, d_inner + gn:].astype(BF16)
    dt = _softplus(dt_raw + dtb_ref[...])
    a_s[...] = dt * (-jnp.exp(alog_ref[...]))
    hrow = lax.broadcasted_iota(jnp.int32, (LANES, d_inner), 0)
    ccol = lax.broadcasted_iota(jnp.int32, (LANES, d_inner), 1)
    expand = ((ccol >> 6) == hrow).astype(BF16)
    xdt_s[...] = xs * _dot_f32_lhs(dt, expand)

    row = lax.broadcasted_iota(jnp.int32, (CHUNK, CHUNK), 0)
    lane = lax.broadcasted_iota(jnp.int32, (CHUNK, CHUNK), 1)
    causal = row >= lane
    tri = causal.astype(BF16)
    lo_half = lane < HEAD_DIM
    lo_half_row = lo_half[0:1, :]
    dexp = dexp_ref[...]

    def chunk_body(c, carry):
        r0 = pl.multiple_of(c * CHUNK, CHUNK)
        rows = pl.ds(r0, CHUNK)
        a_c = a_s[rows, :]
        a_cum = _dot_f32_rhs(tri, a_c)
        a_cum_t = a_cum.T
        a_last = a_cum[CHUNK - 1:CHUNK, :]
        for g in range(N_GROUPS):
            gl = slice(g * D_STATE, (g + 1) * D_STATE)
            cg = c_s[rows, gl]
            bg = b_s[rows, gl]
            cb = _dot_nt(cg, bg)
            ht_g = ht_s[g]
            yoff_g = _dot(cg, ht_g.astype(BF16))
            xw_parts = []
            dec_parts = []
            for j in range(HEADS_PER_GROUP // 2):
                h0 = g * HEADS_PER_GROUP + 2 * j
                h1 = h0 + 1
                pl_ = slice((g * 4 + j) * LANES, (g * 4 + j + 1) * LANES)
                xdt_p = xdt_s[rows, pl_]
                acol0 = jnp.broadcast_to(_col(a_cum, h0), (CHUNK, CHUNK))
                acol1 = jnp.broadcast_to(_col(a_cum, h1), (CHUNK, CHUNK))

                def scores(h, acol):
                    diff = acol - a_cum_t[h:h + 1, :]
                    dec = jnp.exp(jnp.where(causal, diff, -jnp.inf))
                    return (cb * dec).astype(BF16)

                x0 = jnp.where(lo_half, xdt_p, 0.0).astype(BF16)
                x1 = jnp.where(lo_half, 0.0, xdt_p).astype(BF16)
                y_diag = _dot(scores(h0, acol0), x0) + _dot(scores(h1, acol1), x1)
                a_pair = jnp.where(lo_half, acol0, acol1)
                last_pair = jnp.where(lo_half_row, a_last[:, h0:h0 + 1], a_last[:, h1:h1 + 1])
                y_pair = (y_diag + jnp.exp(a_pair) * yoff_g[:, j * LANES:(j + 1) * LANES]
                          + dexp[:, pl_] * xs_s[rows, pl_])
                yb_s[rows, pl_] = y_pair
                xw_parts.append((xdt_p * jnp.exp(last_pair - a_pair)).astype(BF16))
                dec_parts.append(jnp.exp(last_pair))
            xw_g = jnp.concatenate(xw_parts, axis=1)
            dec_g = jnp.concatenate(dec_parts, axis=1)
            ht_s[g] = ht_g * dec_g + _dot_tn(bg, xw_g)
        return carry

    lax.fori_loop(0, n_chunks, chunk_body, 0)

    yg = yb_s[...] * _silu(z_s[...])
    yn = _rms(yg, ng_ref[...]).astype(BF16)
    o = _dot(yn, wout_ref[...])
    y_ref[...] = x + _rms(o, gpost_ref[...])

    @pl.when(t == nt - 1)
    def _():
        for g in range(N_GROUPS):
            hg = ht_s[g].T
            for r in range(HEADS_PER_GROUP):
                hout_ref[g * HEADS_PER_GROUP + r] = hg[r * HEAD_DIM:(r + 1) * HEAD_DIM, :]


def _const_spec(shape):
    nd = len(shape)
    return pl.BlockSpec(shape, lambda *_: (0,) * nd, pipeline_mode=pl.Buffered(1))


def _ssd_prompt_layer(x, gpre, win, wdt, cw, cb, dtb, alog, dexp, ng, wout, gpost, *, tile):
    bsz, seq, d = x.shape
    d_inner = wout.shape[0]
    conv_dim = cw.shape[1]
    gn = N_GROUPS * D_STATE
    heads = d_inner // HEAD_DIM
    kw = cw.shape[0]
    consts = (gpre, win, wdt, cw, cb, dtb, alog, dexp, ng, wout, gpost)
    return pl.pallas_call(
        _ssd_prompt_kernel,
        grid=(bsz, seq // tile),
        in_specs=[pl.BlockSpec((None, tile, d), lambda b, t: (b, t, 0))]
        + [_const_spec(c.shape) for c in consts],
        out_specs=[
            pl.BlockSpec((None, tile, d), lambda b, t: (b, t, 0)),
            pl.BlockSpec((None, heads, HEAD_DIM, D_STATE), lambda b, t: (b, 0, 0, 0)),
            pl.BlockSpec((None, kw - 1, conv_dim), lambda b, t: (b, 0, 0)),
        ],
        out_shape=[
            jax.ShapeDtypeStruct((bsz, seq, d), F32),
            jax.ShapeDtypeStruct((bsz, heads, HEAD_DIM, D_STATE), F32),
            jax.ShapeDtypeStruct((bsz, kw - 1, conv_dim), F32),
        ],
        scratch_shapes=[
            pltpu.VMEM((N_GROUPS, D_STATE, HEADS_PER_GROUP * HEAD_DIM), F32),
            pltpu.VMEM((8, conv_dim), F32),
            pltpu.VMEM((tile, d_inner), F32),
            pltpu.VMEM((tile, d_inner), F32),
            pltpu.VMEM((tile, gn), BF16),
            pltpu.VMEM((tile, gn), BF16),
            pltpu.VMEM((tile, d_inner), F32),
            pltpu.VMEM((tile, LANES), F32),
            pltpu.VMEM((tile, d_inner), F32),
        ],
        compiler_params=pltpu.CompilerParams(
            dimension_semantics=("arbitrary", "arbitrary"),
            vmem_limit_bytes=VMEM_LIMIT_BYTES),
        name="ssd_prompt_layer",
    )(x, *consts)


def _cfm_prompt_kernel(xa_ref, xb_ref, gpre_ref, win_ref, bin_ref,
                       cw_ref, cb_ref, lng_ref, lnb_ref, wout_ref, gpost_ref,
                       y_ref, cout_ref, up_s, ext_s, z_s, acc_s, taps_s, *, tiles_per_seq):
    s = pl.program_id(0)
    n_tiles = pl.num_programs(0) - 1
    tile = xa_ref.shape[0]
    n_cblk = win_ref.shape[0]
    cblk = win_ref.shape[2] // 3
    lanes_per_cblk = cblk // LANES
    kw = cw_ref.shape[1]
    nslab = tile // 8
    nh = kw - 1
    hb = CFM_HALO_SLABS * 8
    nxt = s % 2
    cur = 1 - nxt
    first_of_seq = (jnp.minimum(s, n_tiles - 1) % tiles_per_seq) == 0

    @pl.when(s == 0)
    def _():
        ext_s[...] = jnp.zeros_like(ext_s)
        z_s[...] = jnp.zeros_like(z_s)
        for lb in range(n_cblk * lanes_per_cblk):
            for k in range(kw):
                taps_s[lb, k * 8:(k + 1) * 8, :] = jnp.broadcast_to(cw_ref[lb, k:k + 1, :], (8, LANES))

    ri = lax.broadcasted_iota(jnp.int32, (tile, tile), 0)
    ci = lax.broadcasted_iota(jnp.int32, (tile, tile), 1)
    to_perm = (ci == (ri & 7) * nslab + (ri >> 3)).astype(BF16)
    to_nat = (ri == (ci & 7) * nslab + (ci >> 3)).astype(BF16)
    u = _rms(xa_ref[...], gpre_ref[...]).astype(BF16)
    up_s[0:tile, :] = _dot(to_perm, u).astype(BF16)
    sub = lax.broadcasted_iota(jnp.int32, (nh * 8, LANES), 0) & 7

    def block_body(c, carry):
        lbs = [c * lanes_per_cblk + e for e in range(lanes_per_cblk)]
        prev_tails = [ext_s[cur, lb, hb + (nslab - nh) * 8:hb + tile, :] for lb in lbs]

        def conv_unit(j0, lb):
            accs = [jnp.broadcast_to(cb_ref[lb], (8, LANES))] * CFM_SLAB_GROUP
            for m in range(j0, j0 + CFM_SLAB_GROUP + kw - 1):
                e_m = ext_s[cur, lb, pl.ds((CFM_HALO_SLABS - nh + m) * 8, 8), :]
                for jj in range(CFM_SLAB_GROUP):
                    k = m - j0 - jj
                    if 0 <= k < kw:
                        accs[jj] = accs[jj] + taps_s[lb, pl.ds(k * 8, 8), :] * e_m
            for jj in range(CFM_SLAB_GROUP):
                acc_s[lb, pl.ds((j0 + jj) * 8, 8), :] = accs[jj]
            return accs[-1]

        def gated_input(after):
            up_s[tile:tile + 16, 0:LANES] = jnp.concatenate([after, after], axis=0).astype(BF16)
            return up_s[pl.ds(pl.multiple_of(jnp.minimum(s, 0) * 16, 16), tile), :]

        units = [(j0, lb) for j0 in range(0, nslab, CFM_SLAB_GROUP) for lb in lbs]
        proj = _dot(gated_input(conv_unit(*units[0])), win_ref[c]) + bin_ref[c]
        part = [proj[:, i * LANES:(i + 1) * LANES] for i in range(6)]
        z_s[nxt, c] = jnp.concatenate([part[3], part[5]], axis=1)
        for unit in units[1:]:
            conv_unit(*unit)
        v = jnp.concatenate([part[1] * _sigmoid(part[0]), part[4] * _sigmoid(part[2])], axis=1)
        cout_ref[c] = v.reshape(nslab, 8, cblk)[nslab - nh:, 7, :]
        for e in range(lanes_per_cblk):
            lb = c * lanes_per_cblk + e
            v_e = v[:, e * LANES:(e + 1) * LANES]
            prev_tail = jnp.where(first_of_seq, 0.0, prev_tails[e])
            mixed = jnp.where(sub == 7, prev_tail, v_e[(nslab - nh) * 8:, :]).reshape(nh, 8, LANES)
            halo = jnp.concatenate([mixed[:, 7:8, :], mixed[:, 0:7, :]], axis=1)
            ext_s[nxt, lb, hb - nh * 8:hb, :] = halo.reshape(nh * 8, LANES)
            ext_s[nxt, lb, hb:hb + tile, :] = v_e
        return carry

    lax.fori_loop(0, n_cblk, block_body, 0)

    acc = jnp.concatenate([acc_s[lb] for lb in range(n_cblk * lanes_per_cblk)], axis=1)
    xc = acc - jnp.mean(acc, axis=-1, keepdims=True)
    rstd = lax.rsqrt(jnp.mean(xc * xc, axis=-1, keepdims=True) + EPS)
    o = jnp.zeros(y_ref.shape, F32)
    for c in range(n_cblk):
        cols = slice(c * cblk, (c + 1) * cblk)
        cact = _silu(xc[:, cols] * rstd * lng_ref[:, cols] + lnb_ref[:, cols])
        m_p = (cact * _silu(z_s[cur, c])).astype(BF16)
        m_c = _dot(to_nat, m_p).astype(BF16)
        o = o + _dot(m_c, wout_ref[cols, :])
    y_ref[...] = xb_ref[...] + _rms(o, gpost_ref[...])


def _cfm_prompt_layer(x, gpre, win, bin_, cw, cb, lng, lnb, wout, gpost, *, tile):
    bsz, seq, d = x.shape
    n_cblk = win.shape[0]
    cblk = win.shape[2] // 3
    width = n_cblk * cblk
    n_lblk, kw, _ = cw.shape
    tiles_per_seq = seq // tile
    n_tiles = bsz * tiles_per_seq
    assert kw - 1 <= CFM_HALO_SLABS <= tile // 8 and seq % tile == 0 and n_lblk * LANES == width
    assert cblk == 2 * LANES
    consts = (gpre, win, bin_, cw, cb, lng, lnb, wout, gpost)

    def tile_a(s):
        ta = jnp.minimum(s, n_tiles - 1)
        return ta // tiles_per_seq, ta % tiles_per_seq

    def tile_b(s):
        tb = jnp.maximum(s - 1, 0)
        return tb // tiles_per_seq, tb % tiles_per_seq

    y, cout = pl.pallas_call(
        functools.partial(_cfm_prompt_kernel, tiles_per_seq=tiles_per_seq),
        grid=(n_tiles + 1,),
        in_specs=[pl.BlockSpec((None, tile, d), lambda s: (*tile_a(s), 0)),
                  pl.BlockSpec((None, tile, d), lambda s: (*tile_b(s), 0))]
        + [_const_spec(c.shape) for c in consts],
        out_specs=[
            pl.BlockSpec((None, tile, d), lambda s: (*tile_b(s), 0)),
            pl.BlockSpec((None, n_cblk, kw - 1, cblk), lambda s: (tile_a(s)[0], 0, 0, 0)),
        ],
        out_shape=[
            jax.ShapeDtypeStruct((bsz, seq, d), F32),
            jax.ShapeDtypeStruct((bsz, n_cblk, kw - 1, cblk), F32),
        ],
        scratch_shapes=[
            pltpu.VMEM((tile + 16, d), BF16),
            pltpu.VMEM((2, n_lblk, CFM_HALO_SLABS * 8 + tile, LANES), F32),
            pltpu.VMEM((2, n_cblk, tile, cblk), F32),
            pltpu.VMEM((n_lblk, tile, LANES), F32),
            pltpu.VMEM((n_lblk, kw * 8, LANES), F32),
        ],
        compiler_params=pltpu.CompilerParams(
            dimension_semantics=("arbitrary",),
            vmem_limit_bytes=VMEM_LIMIT_BYTES),
        name="cfm_prompt_layer",
    )(x, x, *consts)
    return y, jnp.transpose(cout, (0, 2, 1, 3)).reshape(bsz, kw - 1, width)


def _single_step_call(kernel, out_shapes, *args, name):
    return pl.pallas_call(
        kernel,
        grid=(1,),
        in_specs=[_const_spec(a.shape) for a in args],
        out_specs=[_const_spec(s.shape) for s in out_shapes],
        out_shape=out_shapes,
        compiler_params=pltpu.CompilerParams(
            dimension_semantics=("arbitrary",),
            vmem_limit_bytes=VMEM_LIMIT_BYTES),
        name=name,
    )(*args)


def _ssd_sample_front_kernel(x_ref, gpre_ref, win_ref, wdt_ref, cst_ref, cw_ref, cb_ref,
                             dtb_ref, alog_ref,
                             z_ref, xs_ref, b_ref, c_ref, xdt_t_ref, da_ref, cst_out_ref):
    d_inner = xs_ref.shape[1]
    gn = N_GROUPS * D_STATE
    kw = cw_ref.shape[0]
    u = _rms(x_ref[...], gpre_ref[...]).astype(BF16)
    z_ref[...] = _dot(u, win_ref[:, 0:d_inner])
    xbc = _dot(u, win_ref[:, d_inner:d_inner + cw_ref.shape[1]])
    dt_raw = _dot(u, wdt_ref[...])

    acc = cb_ref[...] + cw_ref[kw - 1:kw, :] * xbc
    for k in range(kw - 1):
        acc = acc + cw_ref[k:k + 1, :] * cst_ref[k]
    for k in range(kw - 2):
        cst_out_ref[k] = cst_ref[k + 1]
    cst_out_ref[kw - 2] = xbc
    act = _silu(acc)
    xs = act[:, :d_inner]
    xs_ref[...] = xs
    b_ref[...] = act[:, d_inner:d_inner + gn]
    c_ref[...] = act[:, d_inner + gn:]

    dt = _softplus(dt_raw + dtb_ref[...])
    da_ref[...] = jnp.exp(dt * (-jnp.exp(alog_ref[...])))[:, :da_ref.shape[1]]
    hrow = lax.broadcasted_iota(jnp.int32, (LANES, d_inner), 0)
    ccol = lax.broadcasted_iota(jnp.int32, (LANES, d_inner), 1)
    expand = ((ccol >> 6) == hrow).astype(BF16)
    xdt_t_ref[...] = (xs * _dot_f32_lhs(dt, expand)).T.astype(BF16)


def _ssd_sample_state_kernel(da_ref, h_ref, xdt_t_ref, ball_ref, c_ref, hn_ref, yt_ref):
    step = pl.program_id(0)
    tok_block = h_ref.shape[0]
    ntok = ball_ref.shape[0]
    rpg = HEADS_PER_GROUP * HEAD_DIM
    tok_row = lax.broadcasted_iota(jnp.int32, (ntok, D_STATE), 0)
    tok_lane = lax.broadcasted_iota(jnp.int32, (rpg, ntok), 1)
    blk_row = lax.broadcasted_iota(jnp.int32, (tok_block, D_STATE), 0)

    @pl.when(step == 0)
    def _():
        yt_ref[...] = jnp.zeros_like(yt_ref)

    def token_body(j, carry):
        tok = step * tok_block + j
        for g in range(N_GROUPS):
            gl = slice(g * D_STATE, (g + 1) * D_STATE)
            gr = slice(g * rpg, (g + 1) * rpg)
            b_tok = jnp.where(tok_row == tok, ball_ref[:, gl], 0.0).astype(BF16)
            outer = _dot(xdt_t_ref[gr, :], b_tok)
            scaled = [h_ref[j, (g * HEADS_PER_GROUP + r) * HEAD_DIM:(g * HEADS_PER_GROUP + r + 1) * HEAD_DIM, :]
                      * da_ref[tok, g * HEADS_PER_GROUP + r] for r in range(HEADS_PER_GROUP)]
            hn_g = jnp.concatenate(scaled, axis=0) + outer
            hn_ref[j, gr, :] = hn_g
            c_tok = jnp.sum(jnp.where(blk_row == j, c_ref[:, gl], 0.0), axis=0, keepdims=True)
            ycol = jnp.sum(hn_g * c_tok, axis=1, keepdims=True)
            yt_ref[gr, :] = jnp.where(tok_lane == tok, ycol, yt_ref[gr, :])
        return carry

    lax.fori_loop(0, tok_block, token_body, 0)


def _ssd_sample_state(da, h0, xdt_t, b_all, c_all, *, tok_block):
    ntok, rows, n = h0.shape
    return pl.pallas_call(
        _ssd_sample_state_kernel,
        grid=(ntok // tok_block,),
        in_specs=[
            pl.BlockSpec(memory_space=pltpu.SMEM),
            pl.BlockSpec((tok_block, rows, n), lambda i: (i, 0, 0)),
            _const_spec(xdt_t.shape),
            _const_spec(b_all.shape),
            pl.BlockSpec((tok_block, c_all.shape[1]), lambda i: (i, 0)),
        ],
        out_specs=[
            pl.BlockSpec((tok_block, rows, n), lambda i: (i, 0, 0)),
            pl.BlockSpec((rows, ntok), lambda i: (0, 0)),
        ],
        out_shape=[
            jax.ShapeDtypeStruct(h0.shape, F32),
            jax.ShapeDtypeStruct((rows, ntok), F32),
        ],
        compiler_params=pltpu.CompilerParams(
            dimension_semantics=("arbitrary",),
            vmem_limit_bytes=VMEM_LIMIT_BYTES),
        name="ssd_sample_state",
    )(da, h0, xdt_t, b_all, c_all)


def _ssd_sample_tail_kernel(yt_ref, xs_ref, z_ref, x_ref, dexp_ref, ng_ref, wout_ref, gpost_ref, o_ref):
    y = yt_ref[...].T + dexp_ref[...] * xs_ref[...]
    yg = y * _silu(z_ref[...])
    yn = _rms(yg, ng_ref[...]).astype(BF16)
    o = _dot(yn, wout_ref[...])
    o_ref[...] = x_ref[...] + _rms(o, gpost_ref[...])


def _cfm_sample_front_kernel(x_ref, gpre_ref, w_ref, bv_ref, bg_ref, bz_ref,
                             v_ref, z_ref):
    u = _rms(x_ref[...], gpre_ref[...]).astype(BF16)
    width = v_ref.shape[1]
    val = _dot(u, w_ref[:, 0:width]) + bv_ref[...]
    gate = _dot(u, w_ref[:, width:2 * width]) + bg_ref[...]
    z_ref[...] = _dot(u, w_ref[:, 2 * width:]) + bz_ref[...]
    v_ref[...] = val * _sigmoid(gate)


def _cfm_sample_conv_kernel(cst_ref, v_ref, cw_ref, cb_ref, c_ref, cst_out_ref):
    kw = cw_ref.shape[0]
    v = v_ref[...]
    acc = cb_ref[...] + cw_ref[kw - 1:kw, :] * v
    for k in range(kw - 1):
        acc = acc + cw_ref[k:k + 1, :] * cst_ref[k]
    c_ref[...] = acc
    for k in range(kw - 2):
        cst_out_ref[k] = cst_ref[k + 1]
    cst_out_ref[kw - 2] = v


def _cfm_sample_conv(cst, v, cw, cb, *, tok_block):
    taps, ntok, width = cst.shape
    return pl.pallas_call(
        _cfm_sample_conv_kernel,
        grid=(ntok // tok_block,),
        in_specs=[
            pl.BlockSpec((taps, tok_block, width), lambda i: (0, i, 0)),
            pl.BlockSpec((tok_block, width), lambda i: (i, 0)),
            _const_spec(cw.shape),
            _const_spec(cb.shape),
        ],
        out_specs=[
            pl.BlockSpec((tok_block, width), lambda i: (i, 0)),
            pl.BlockSpec((taps, tok_block, width), lambda i: (0, i, 0)),
        ],
        out_shape=[
            jax.ShapeDtypeStruct((ntok, width), F32),
            jax.ShapeDtypeStruct(cst.shape, F32),
        ],
        compiler_params=pltpu.CompilerParams(
            dimension_semantics=("arbitrary",),
            vmem_limit_bytes=VMEM_LIMIT_BYTES),
        name="cfm_sample_conv",
    )(cst, v, cw, cb)


def _cfm_sample_tail_kernel(c_ref, z_ref, x_ref, lng_ref, lnb_ref, wout_ref, gpost_ref, o_ref):
    c = _silu(_layer_norm(c_ref[...], lng_ref[...], lnb_ref[...]))
    o = _dot((c * _silu(z_ref[...])).astype(BF16), wout_ref[...])
    o_ref[...] = x_ref[...] + _rms(o, gpost_ref[...])


def _row(v):
    return v.reshape(1, -1)


def _pad_lanes(a):
    pad = (-a.shape[-1]) % LANES
    return jnp.pad(a, [(0, 0)] * (a.ndim - 1) + [(0, pad)])


def kernel(x_prompt, x_sample, state_ssm, state_conv_ssm, state_conv_cfm, g_pre, g_post, ssm_w_in, ssm_conv_w, ssm_conv_b, ssm_dt_bias, ssm_A_log, ssm_D, ssm_norm_g, ssm_w_out, cfm_w_in, cfm_b_in, cfm_conv_w, cfm_conv_b, cfm_ln_g, cfm_ln_b, cfm_w_out):
    d_inner = ssm_w_out.shape[1]
    conv_dim = ssm_conv_w.shape[2]
    heads = ssm_D.shape[1]
    width = cfm_w_out.shape[1]
    ntok = x_sample.shape[0]
    d = x_sample.shape[2]

    w_in = ssm_w_in[0]
    win0 = w_in.astype(BF16)
    wdt = _pad_lanes(w_in[:, d_inner + conv_dim:]).astype(BF16)
    cw0, cb0 = ssm_conv_w[0], _row(ssm_conv_b[0])
    dtb = _pad_lanes(_row(ssm_dt_bias[0]))
    alog = _pad_lanes(_row(ssm_A_log[0]))
    dexp = _row(jnp.repeat(ssm_D[0], d_inner // heads))
    ng = _row(ssm_norm_g[0])
    wout0 = ssm_w_out[0].astype(BF16)
    gpre0, gpost0 = _row(g_pre[0]), _row(g_post[0])

    w1 = cfm_w_in[0]
    w1b = w1.astype(BF16)
    wv, wg, wz1 = (w1b[:, i * width:(i + 1) * width] for i in range(3))
    bv, bg, bz = (_row(cfm_b_in[0][i * width:(i + 1) * width]) for i in range(3))
    cw1, cb1 = cfm_conv_w[0], _row(cfm_conv_b[0])
    lng, lnb = _row(cfm_ln_g[0]), _row(cfm_ln_b[0])
    wout1 = cfm_w_out[0].astype(BF16)
    gpre1, gpost1 = _row(g_pre[1]), _row(g_post[1])

    xp1, ssm_p, cssm_p = _ssd_prompt_layer(
        x_prompt, gpre0, win0, wdt, cw0, cb0, dtb, alog, dexp, ng, wout0, gpost0, tile=256)
    def col_blocks(w):
        return jnp.transpose(w.reshape(w.shape[0], width // CFM_CBLK, CFM_CBLK), (1, 0, 2))

    def mixed_blocks(g, v, z):
        g, v, z = col_blocks(g), col_blocks(v), col_blocks(z)
        lo, hi = slice(0, LANES), slice(LANES, 2 * LANES)
        return jnp.concatenate([g[..., lo], v[..., lo], g[..., hi], z[..., lo], v[..., hi], z[..., hi]], axis=2)

    def lane_blocks(a):
        return jnp.transpose(a.reshape(a.shape[0], width // LANES, LANES), (1, 0, 2))

    xp2, ccfm_p = _cfm_prompt_layer(
        xp1, gpre1,
        mixed_blocks(wg, wv, wz1), mixed_blocks(bg, bv, bz),
        lane_blocks(cw1), lane_blocks(cb1), lng, lnb, wout1, gpost1, tile=256)

    xs0 = x_sample.reshape(ntok, d)
    cst0 = jnp.transpose(state_conv_ssm[0], (1, 0, 2))
    gn = N_GROUPS * D_STATE
    sds = jax.ShapeDtypeStruct
    z_s, xs_s, b_s, c_s, xdt_t, da, cst0_new = _single_step_call(
        _ssd_sample_front_kernel,
        [sds((ntok, d_inner), F32), sds((ntok, d_inner), F32), sds((ntok, gn), F32),
         sds((ntok, gn), F32), sds((d_inner, ntok), BF16), sds((ntok, heads), F32),
         sds(cst0.shape, F32)],
        xs0, gpre0, win0, wdt, cst0, cw0, cb0, dtb, alog,
        name="ssd_sample_front")
    h0 = state_ssm[0].reshape(ntok, heads * HEAD_DIM, D_STATE)
    h_new, y_t = _ssd_sample_state(da, h0, xdt_t, b_s, c_s, tok_block=8)
    (xs1,) = _single_step_call(
        _ssd_sample_tail_kernel, [sds((ntok, d), F32)],
        y_t, xs_s, z_s, xs0, dexp, ng, wout0, gpost0, name="ssd_sample_tail")
    v_s, zc_s = _single_step_call(
        _cfm_sample_front_kernel, [sds((ntok, width), F32), sds((ntok, width), F32)],
        xs1, gpre1, w1b, bv, bg, bz, name="cfm_sample_front")
    c_conv, cst1_new = _cfm_sample_conv(
        jnp.transpose(state_conv_cfm[0], (1, 0, 2)), v_s, cw1, cb1, tok_block=16)
    (xs2,) = _single_step_call(
        _cfm_sample_tail_kernel, [sds((ntok, d), F32)],
        c_conv, zc_s, xs1, lng, lnb, wout1, gpost1, name="cfm_sample_tail")

    return (xp2,
            xs2.reshape(x_sample.shape),
            ssm_p[None],
            cssm_p[None],
            ccfm_p[None],
            h_new.reshape(state_ssm.shape),
            jnp.transpose(cst0_new, (1, 0, 2))[None],
            jnp.transpose(cst1_new, (1, 0, 2))[None])
```

```python
import functools

import jax
import jax.numpy as jnp
from jax import lax
from jax.experimental import pallas as pl
from jax.experimental.pallas import tpu as pltpu

F32 = jnp.float32
BF16 = jnp.bfloat16

EPS = 1e-6
LANES = 128
CHUNK = 128
HEAD_DIM = 64
D_STATE = 128
N_GROUPS = 4
HEADS_PER_GROUP = 8
CFM_HALO_SLABS = 32
CFM_CBLK = 256
CFM_SLAB_GROUP = 8
VMEM_LIMIT_BYTES = 56 * 1024 * 1024


def _dot(a, b):
    return jnp.dot(a, b, preferred_element_type=F32)


def _dot_nt(a, b):
    return lax.dot_general(a, b, (((1,), (1,)), ((), ())), preferred_element_type=F32)


def _dot_tn(a, b):
    return lax.dot_general(a, b, (((0,), (0,)), ((), ())), preferred_element_type=F32)


def _rms(x, g):
    return x * lax.rsqrt(jnp.mean(x * x, axis=-1, keepdims=True) + EPS) * g


def _layer_norm(x, g, b):
    mu = jnp.mean(x, axis=-1, keepdims=True)
    xc = x - mu
    return xc * lax.rsqrt(jnp.mean(xc * xc, axis=-1, keepdims=True) + EPS) * g + b


def _sigmoid(x):
    return 1.0 / (1.0 + jnp.exp(-x))


def _silu(x):
    h = 0.5 * x
    return h + h * jnp.tanh(h)


def _softplus(x):
    return jnp.maximum(x, 0.0) + jnp.log1p(jnp.exp(-jnp.abs(x)))


def _dot_f32_rhs(l_bf16, a):
    hi = a.astype(BF16)
    r1 = a - hi.astype(F32)
    mid = r1.astype(BF16)
    lo = (r1 - mid.astype(F32)).astype(BF16)
    return _dot(l_bf16, hi) + _dot(l_bf16, mid) + _dot(l_bf16, lo)


def _dot_f32_lhs(a, r_bf16):
    hi = a.astype(BF16)
    r1 = a - hi.astype(F32)
    mid = r1.astype(BF16)
    lo = (r1 - mid.astype(F32)).astype(BF16)
    return _dot(hi, r_bf16) + _dot(mid, r_bf16) + _dot(lo, r_bf16)


def _col(a, j):
    return a[:, j:j + 1]


def _ssd_prompt_kernel(x_ref, gpre_ref, win_ref, wdt_ref, cw_ref, cb_ref,
                       dtb_ref, alog_ref, dexp_ref, ng_ref, wout_ref, gpost_ref,
                       y_ref, hout_ref, cout_ref,
                       ht_s, tail_s, xs_s, xdt_s, b_s, c_s, z_s, a_s, yb_s):
    t = pl.program_id(1)
    nt = pl.num_programs(1)
    tile = x_ref.shape[0]
    n_chunks = tile // CHUNK
    d_inner = xs_s.shape[1]
    gn = N_GROUPS * D_STATE
    kw = cw_ref.shape[0]
    halo = 8

    @pl.when(t == 0)
    def _():
        ht_s[...] = jnp.zeros_like(ht_s)
        tail_s[...] = jnp.zeros_like(tail_s)

    x = x_ref[...]
    u = _rms(x, gpre_ref[...]).astype(BF16)
    z_s[...] = _dot(u, win_ref[:, 0:d_inner])
    xbc = _dot(u, win_ref[:, d_inner:d_inner + cw_ref.shape[1]])
    dt_raw = _dot(u, wdt_ref[...])

    conv_dim = xbc.shape[1]
    rows8 = jnp.concatenate([tail_s[...].reshape(1, halo, conv_dim),
                             xbc.reshape(tile // halo, halo, conv_dim)], axis=0)
    sub = lax.broadcasted_iota(jnp.int32, (tile // halo, halo, conv_dim), 1)
    acc = cb_ref[...] + cw_ref[kw - 1:kw, :] * xbc
    for sh in range(1, kw):
        rot = jnp.concatenate([rows8[:, halo - sh:, :], rows8[:, :halo - sh, :]], axis=1)
        win = jnp.where(sub < sh, rot[:-1], rot[1:]).reshape(tile, conv_dim)
        acc = acc + cw_ref[kw - 1 - sh:kw - sh, :] * win
    last8 = xbc[tile - halo:, :]
    cout_ref[...] = last8[halo - (kw - 1):, :]
    tail_s[...] = last8
    act = _silu(acc)
    xs = act[:, :d_inner]
    xs_s[...] = xs
    b_s[...] = act[:, d_inner:d_inner + gn].astype(BF16)
    c_s[...] = act[:, d_inner + gn:].astype(BF16)
    dt = _softplus(dt_raw + dtb_ref[...])
    a_s[...] = dt * (-jnp.exp(alog_ref[...]))
    hrow = lax.broadcasted_iota(jnp.int32, (LANES, d_inner), 0)
    ccol = lax.broadcasted_iota(jnp.int32, (LANES, d_inner), 1)
    expand = ((ccol >> 6) == hrow).astype(BF16)
    xdt_s[...] = xs * _dot_f32_lhs(dt, expand)

    row = lax.broadcasted_iota(jnp.int32, (CHUNK, CHUNK), 0)
    lane = lax.broadcasted_iota(jnp.int32, (CHUNK, CHUNK), 1)
    causal = row >= lane
    tri = causal.astype(BF16)
    lo_half = lane < HEAD_DIM
    lo_half_row = lo_half[0:1, :]
    dexp = dexp_ref[...]

    def chunk_body(c, carry):
        r0 = pl.multiple_of(c * CHUNK, CHUNK)
        rows = pl.ds(r0, CHUNK)
        a_c = a_s[rows, :]
        a_cum = _dot_f32_rhs(tri, a_c)
        a_cum_t = a_cum.T
        a_last = a_cum[CHUNK - 1:CHUNK, :]
        for g in range(N_GROUPS):
            gl = slice(g * D_STATE, (g + 1) * D_STATE)
            cg = c_s[rows, gl]
            bg = b_s[rows, gl]
            cb = _dot_nt(cg, bg)
            ht_g = ht_s[g]
            yoff_g = _dot(cg, ht_g.astype(BF16))
            xw_parts = []
            dec_parts = []
            for j in range(HEADS_PER_GROUP // 2):
                h0 = g * HEADS_PER_GROUP + 2 * j
                h1 = h0 + 1
                pl_ = slice((g * 4 + j) * LANES, (g * 4 + j + 1) * LANES)
                xdt_p = xdt_s[rows, pl_]
                acol0 = jnp.broadcast_to(_col(a_cum, h0), (CHUNK, CHUNK))
                acol1 = jnp.broadcast_to(_col(a_cum, h1), (CHUNK, CHUNK))

                def scores(h, acol):
                    diff = acol - a_cum_t[h:h + 1, :]
                    dec = jnp.exp(jnp.where(causal, diff, -jnp.inf))
                    return (cb * dec).astype(BF16)

                x0 = jnp.where(lo_half, xdt_p, 0.0).astype(BF16)
                x1 = jnp.where(lo_half, 0.0, xdt_p).astype(BF16)
                y_diag = _dot(scores(h0, acol0), x0) + _dot(scores(h1, acol1), x1)
                a_pair = jnp.where(lo_half, acol0, acol1)
                last_pair = jnp.where(lo_half_row, a_last[:, h0:h0 + 1], a_last[:, h1:h1 + 1])
                y_pair = (y_diag + jnp.exp(a_pair) * yoff_g[:, j * LANES:(j + 1) * LANES]
                          + dexp[:, pl_] * xs_s[rows, pl_])
                yb_s[rows, pl_] = y_pair
                xw_parts.append((xdt_p * jnp.exp(last_pair - a_pair)).astype(BF16))
                dec_parts.append(jnp.exp(last_pair))
            xw_g = jnp.concatenate(xw_parts, axis=1)
            dec_g = jnp.concatenate(dec_parts, axis=1)
            ht_s[g] = ht_g * dec_g + _dot_tn(bg, xw_g)
        return carry

    lax.fori_loop(0, n_chunks, chunk_body, 0)

    yg = yb_s[...] * _silu(z_s[...])
    yn = _rms(yg, ng_ref[...]).astype(BF16)
    o = _dot(yn, wout_ref[...])
    y_ref[...] = x + _rms(o, gpost_ref[...])

    @pl.when(t == nt - 1)
    def _():
        for g in range(N_GROUPS):
            hg = ht_s[g].T
            for r in range(HEADS_PER_GROUP):
                hout_ref[g * HEADS_PER_GROUP + r] = hg[r * HEAD_DIM:(r + 1) * HEAD_DIM, :]


def _const_spec(shape):
    nd = len(shape)
    return pl.BlockSpec(shape, lambda *_: (0,) * nd, pipeline_mode=pl.Buffered(1))


def _ssd_prompt_layer(x, gpre, win, wdt, cw, cb, dtb, alog, dexp, ng, wout, gpost, *, tile):
    bsz, seq, d = x.shape
    d_inner = wout.shape[0]
    conv_dim = cw.shape[1]
    gn = N_GROUPS * D_STATE
    heads = d_inner // HEAD_DIM
    kw = cw.shape[0]
    consts = (gpre, win, wdt, cw, cb, dtb, alog, dexp, ng, wout, gpost)
    return pl.pallas_call(
        _ssd_prompt_kernel,
        grid=(bsz, seq // tile),
        in_specs=[pl.BlockSpec((None, tile, d), lambda b, t: (b, t, 0))]
        + [_const_spec(c.shape) for c in consts],
        out_specs=[
            pl.BlockSpec((None, tile, d), lambda b, t: (b, t, 0)),
            pl.BlockSpec((None, heads, HEAD_DIM, D_STATE), lambda b, t: (b, 0, 0, 0)),
            pl.BlockSpec((None, kw - 1, conv_dim), lambda b, t: (b, 0, 0)),
        ],
        out_shape=[
            jax.ShapeDtypeStruct((bsz, seq, d), F32),
            jax.ShapeDtypeStruct((bsz, heads, HEAD_DIM, D_STATE), F32),
            jax.ShapeDtypeStruct((bsz, kw - 1, conv_dim), F32),
        ],
        scratch_shapes=[
            pltpu.VMEM((N_GROUPS, D_STATE, HEADS_PER_GROUP * HEAD_DIM), F32),
            pltpu.VMEM((8, conv_dim), F32),
            pltpu.VMEM((tile, d_inner), F32),
            pltpu.VMEM((tile, d_inner), F32),
            pltpu.VMEM((tile, gn), BF16),
            pltpu.VMEM((tile, gn), BF16),
            pltpu.VMEM((tile, d_inner), F32),
            pltpu.VMEM((tile, LANES), F32),
            pltpu.VMEM((tile, d_inner), F32),
        ],
        compiler_params=pltpu.CompilerParams(
            dimension_semantics=("arbitrary", "arbitrary"),
            vmem_limit_bytes=VMEM_LIMIT_BYTES),
        name="ssd_prompt_layer",
    )(x, *consts)


def _cfm_prompt_kernel(xa_ref, xb_ref, gpre_ref, win_ref, bin_ref,
                       cw_ref, cb_ref, lng_ref, lnb_ref, wout_ref, gpost_ref,
                       y_ref, cout_ref, up_s, ext_s, z_s, acc_s, taps_s, *, tiles_per_seq):
    s = pl.program_id(0)
    n_tiles = pl.num_programs(0) - 1
    tile = xa_ref.shape[0]
    n_cblk = win_ref.shape[0]
    cblk = win_ref.shape[2] // 3
    lanes_per_cblk = cblk // LANES
    kw = cw_ref.shape[1]
    nslab = tile // 8
    nh = kw - 1
    hb = CFM_HALO_SLABS * 8
    nxt = s % 2
    cur = 1 - nxt
    first_of_seq = (jnp.minimum(s, n_tiles - 1) % tiles_per_seq) == 0

    @pl.when(s == 0)
    def _():
        ext_s[...] = jnp.zeros_like(ext_s)
        z_s[...] = jnp.zeros_like(z_s)
        for lb in range(n_cblk * lanes_per_cblk):
            for k in range(kw):
                taps_s[lb, k * 8:(k + 1) * 8, :] = jnp.broadcast_to(cw_ref[lb, k:k + 1, :], (8, LANES))

    ri = lax.broadcasted_iota(jnp.int32, (tile, tile), 0)
    ci = lax.broadcasted_iota(jnp.int32, (tile, tile), 1)
    to_perm = (ci == (ri & 7) * nslab + (ri >> 3)).astype(BF16)
    to_nat = (ri == (ci & 7) * nslab + (ci >> 3)).astype(BF16)
    u = _rms(xa_ref[...], gpre_ref[...]).astype(BF16)
    up_s[0:tile, :] = _dot(to_perm, u).astype(BF16)
    sub = lax.broadcasted_iota(jnp.int32, (nh * 8, LANES), 0) & 7

    def block_body(c, carry):
        lbs = [c * lanes_per_cblk + e for e in range(lanes_per_cblk)]
        prev_tails = [ext_s[cur, lb, hb + (nslab - nh) * 8:hb + tile, :] for lb in lbs]

        def conv_unit(j0, lb):
            accs = [jnp.broadcast_to(cb_ref[lb], (8, LANES))] * CFM_SLAB_GROUP
            for m in range(j0, j0 + CFM_SLAB_GROUP + kw - 1):
                e_m = ext_s[cur, lb, pl.ds((CFM_HALO_SLABS - nh + m) * 8, 8), :]
                for jj in range(CFM_SLAB_GROUP):
                    k = m - j0 - jj
                    if 0 <= k < kw:
                        accs[jj] = accs[jj] + taps_s[lb, pl.ds(k * 8, 8), :] * e_m
            for jj in range(CFM_SLAB_GROUP):
                acc_s[lb, pl.ds((j0 + jj) * 8, 8), :] = accs[jj]
            return accs[-1]

        def gated_input(after):
            up_s[tile:tile + 16, 0:LANES] = jnp.concatenate([after, after], axis=0).astype(BF16)
            return up_s[pl.ds(pl.multiple_of(jnp.minimum(s, 0) * 16, 16), tile), :]

        units = [(j0, lb) for j0 in range(0, nslab, CFM_SLAB_GROUP) for lb in lbs]
        proj = _dot(gated_input(conv_unit(*units[0])), win_ref[c]) + bin_ref[c]
        part = [proj[:, i * LANES:(i + 1) * LANES] for i in range(6)]
        z_s[nxt, c] = jnp.concatenate([part[3], part[5]], axis=1)
        for unit in units[1:]:
            conv_unit(*unit)
        v = jnp.concatenate([part[1] * _sigmoid(part[0]), part[4] * _sigmoid(part[2])], axis=1)
        cout_ref[c] = v.reshape(nslab, 8, cblk)[nslab - nh:, 7, :]
        for e in range(lanes_per_cblk):
            lb = c * lanes_per_cblk + e
            v_e = v[:, e * LANES:(e + 1) * LANES]
            prev_tail = jnp.where(first_of_seq, 0.0, prev_tails[e])
            mixed = jnp.where(sub == 7, prev_tail, v_e[(nslab - nh) * 8:, :]).reshape(nh, 8, LANES)
            halo = jnp.concatenate([mixed[:, 7:8, :], mixed[:, 0:7, :]], axis=1)
            ext_s[nxt, lb, hb - nh * 8:hb, :] = halo.reshape(nh * 8, LANES)
            ext_s[nxt, lb, hb:hb + tile, :] = v_e
        return carry

    lax.fori_loop(0, n_cblk, block_body, 0)

    acc = jnp.concatenate([acc_s[lb] for lb in range(n_cblk * lanes_per_cblk)], axis=1)
    xc = acc - jnp.mean(acc, axis=-1, keepdims=True)
    rstd = lax.rsqrt(jnp.mean(xc * xc, axis=-1, keepdims=True) + EPS)
    o = jnp.zeros(y_ref.shape, F32)
    for c in range(n_cblk):
        cols = slice(c * cblk, (c + 1) * cblk)
        cact = _silu(xc[:, cols] * rstd * lng_ref[:, cols] + lnb_ref[:, cols])
        m_p = (cact * _silu(z_s[cur, c])).astype(BF16)
        m_c = _dot(to_nat, m_p).astype(BF16)
        o = o + _dot(m_c, wout_ref[cols, :])
    y_ref[...] = xb_ref[...] + _rms(o, gpost_ref[...])


def _cfm_prompt_layer(x, gpre, win, bin_, cw, cb, lng, lnb, wout, gpost, *, tile):
    bsz, seq, d = x.shape
    n_cblk = win.shape[0]
    cblk = win.shape[2] // 3
    width = n_cblk * cblk
    n_lblk, kw, _ = cw.shape
    tiles_per_seq = seq // tile
    n_tiles = bsz * tiles_per_seq
    assert kw - 1 <= CFM_HALO_SLABS <= tile // 8 and seq % tile == 0 and n_lblk * LANES == width
    assert cblk == 2 * LANES
    consts = (gpre, win, bin_, cw, cb, lng, lnb, wout, gpost)

    def tile_a(s):
        ta = jnp.minimum(s, n_tiles - 1)
        return ta // tiles_per_seq, ta % tiles_per_seq

    def tile_b(s):
        tb = jnp.maximum(s - 1, 0)
        return tb // tiles_per_seq, tb % tiles_per_seq

    y, cout = pl.pallas_call(
        functools.partial(_cfm_prompt_kernel, tiles_per_seq=tiles_per_seq),
        grid=(n_tiles + 1,),
        in_specs=[pl.BlockSpec((None, tile, d), lambda s: (*tile_a(s), 0)),
                  pl.BlockSpec((None, tile, d), lambda s: (*tile_b(s), 0))]
        + [_const_spec(c.shape) for c in consts],
        out_specs=[
            pl.BlockSpec((None, tile, d), lambda s: (*tile_b(s), 0)),
            pl.BlockSpec((None, n_cblk, kw - 1, cblk), lambda s: (tile_a(s)[0], 0, 0, 0)),
        ],
        out_shape=[
            jax.ShapeDtypeStruct((bsz, seq, d), F32),
            jax.ShapeDtypeStruct((bsz, n_cblk, kw - 1, cblk), F32),
        ],
        scratch_shapes=[
            pltpu.VMEM((tile + 16, d), BF16),
            pltpu.VMEM((2, n_lblk, CFM_HALO_SLABS * 8 + tile, LANES), F32),
            pltpu.VMEM((2, n_cblk, tile, cblk), F32),
            pltpu.VMEM((n_lblk, tile, LANES), F32),
            pltpu.VMEM((n_lblk, kw * 8, LANES), F32),
        ],
        compiler_params=pltpu.CompilerParams(
            dimension_semantics=("arbitrary",),
            vmem_limit_bytes=VMEM_LIMIT_BYTES),
        name="cfm_prompt_layer",
    )(x, x, *consts)
    return y, jnp.transpose(cout, (0, 2, 1, 3)).reshape(bsz, kw - 1, width)


def _single_step_call(kernel, out_shapes, *args, name):
    return pl.pallas_call(
        kernel,
        grid=(1,),
        in_specs=[_const_spec(a.shape) for a in args],
        out_specs=[_const_spec(s.shape) for s in out_shapes],
        out_shape=out_shapes,
        compiler_params=pltpu.CompilerParams(
            dimension_semantics=("arbitrary",),
            vmem_limit_bytes=VMEM_LIMIT_BYTES),
        name=name,
    )(*args)


def _ssd_sample_front_kernel(x_ref, gpre_ref, win_ref, wdt_ref, cst_ref, cw_ref, cb_ref,
                             dtb_ref, alog_ref,
                             z_ref, xs_ref, b_ref, c_ref, xdt_t_ref, da_ref, cst_out_ref):
    d_inner = xs_ref.shape[1]
    gn = N_GROUPS * D_STATE
    kw = cw_ref.shape[0]
    u = _rms(x_ref[...], gpre_ref[...]).astype(BF16)
    z_ref[...] = _dot(u, win_ref[:, 0:d_inner])
    xbc = _dot(u, win_ref[:, d_inner:d_inner + cw_ref.shape[1]])
    dt_raw = _dot(u, wdt_ref[...])

    acc = cb_ref[...] + cw_ref[kw - 1:kw, :] * xbc
    for k in range(kw - 1):
        acc = acc + cw_ref[k:k + 1, :] * cst_ref[k]
    for k in range(kw - 2):
        cst_out_ref[k] = cst_ref[k + 1]
    cst_out_ref[kw - 2] = xbc
    act = _silu(acc)
    xs = act[:, :d_inner]
    xs_ref[...] = xs
    b_ref[...] = act[:, d_inner:d_inner + gn]
    c_ref[...] = act[:, d_inner + gn:]

    dt = _softplus(dt_raw + dtb_ref[...])
    da_ref[...] = jnp.exp(dt * (-jnp.exp(alog_ref[...])))[:, :da_ref.shape[1]]
    hrow = lax.broadcasted_iota(jnp.int32, (LANES, d_inner), 0)
    ccol = lax.broadcasted_iota(jnp.int32, (LANES, d_inner), 1)
    expand = ((ccol >> 6) == hrow).astype(BF16)
    xdt_t_ref[...] = (xs * _dot_f32_lhs(dt, expand)).T.astype(BF16)


def _ssd_sample_state_kernel(da_ref, h_ref, xdt_t_ref, ball_ref, c_ref, hn_ref, yt_ref):
    step = pl.program_id(0)
    tok_block = h_ref.shape[0]
    ntok = ball_ref.shape[0]
    rpg = HEADS_PER_GROUP * HEAD_DIM
    tok_row = lax.broadcasted_iota(jnp.int32, (ntok, D_STATE), 0)
    tok_lane = lax.broadcasted_iota(jnp.int32, (rpg, ntok), 1)
    blk_row = lax.broadcasted_iota(jnp.int32, (tok_block, D_STATE), 0)

    @pl.when(step == 0)
    def _():
        yt_ref[...] = jnp.zeros_like(yt_ref)

    def token_body(j, carry):
        tok = step * tok_block + j
        for g in range(N_GROUPS):
            gl = slice(g * D_STATE, (g + 1) * D_STATE)
            gr = slice(g * rpg, (g + 1) * rpg)
            b_tok = jnp.where(tok_row == tok, ball_ref[:, gl], 0.0).astype(BF16)
            outer = _dot(xdt_t_ref[gr, :], b_tok)
            scaled = [h_ref[j, (g * HEADS_PER_GROUP + r) * HEAD_DIM:(g * HEADS_PER_GROUP + r + 1) * HEAD_DIM, :]
                      * da_ref[tok, g * HEADS_PER_GROUP + r] for r in range(HEADS_PER_GROUP)]
            hn_g = jnp.concatenate(scaled, axis=0) + outer
            hn_ref[j, gr, :] = hn_g
            c_tok = jnp.sum(jnp.where(blk_row == j, c_ref[:, gl], 0.0), axis=0, keepdims=True)
            ycol = jnp.sum(hn_g * c_tok, axis=1, keepdims=True)
            yt_ref[gr, :] = jnp.where(tok_lane == tok, ycol, yt_ref[gr, :])
        return carry

    lax.fori_loop(0, tok_block, token_body, 0)


def _ssd_sample_state(da, h0, xdt_t, b_all, c_all, *, tok_block):
    ntok, rows, n = h0.shape
    return pl.pallas_call(
        _ssd_sample_state_kernel,
        grid=(ntok // tok_block,),
        in_specs=[
            pl.BlockSpec(memory_space=pltpu.SMEM),
            pl.BlockSpec((tok_block, rows, n), lambda i: (i, 0, 0)),
            _const_spec(xdt_t.shape),
            _const_spec(b_all.shape),
            pl.BlockSpec((tok_block, c_all.shape[1]), lambda i: (i, 0)),
        ],
        out_specs=[
            pl.BlockSpec((tok_block, rows, n), lambda i: (i, 0, 0)),
            pl.BlockSpec((rows, ntok), lambda i: (0, 0)),
        ],
        out_shape=[
            jax.ShapeDtypeStruct(h0.shape, F32),
            jax.ShapeDtypeStruct((rows, ntok), F32),
        ],
        compiler_params=pltpu.CompilerParams(
            dimension_semantics=("arbitrary",),
            vmem_limit_bytes=VMEM_LIMIT_BYTES),
        name="ssd_sample_state",
    )(da, h0, xdt_t, b_all, c_all)


def _ssd_sample_tail_kernel(yt_ref, xs_ref, z_ref, x_ref, dexp_ref, ng_ref, wout_ref, gpost_ref, o_ref):
    y = yt_ref[...].T + dexp_ref[...] * xs_ref[...]
    yg = y * _silu(z_ref[...])
    yn = _rms(yg, ng_ref[...]).astype(BF16)
    o = _dot(yn, wout_ref[...])
    o_ref[...] = x_ref[...] + _rms(o, gpost_ref[...])


def _cfm_sample_front_kernel(x_ref, gpre_ref, w_ref, bv_ref, bg_ref, bz_ref,
                             v_ref, z_ref):
    u = _rms(x_ref[...], gpre_ref[...]).astype(BF16)
    width = v_ref.shape[1]
    val = _dot(u, w_ref[:, 0:width]) + bv_ref[...]
    gate = _dot(u, w_ref[:, width:2 * width]) + bg_ref[...]
    z_ref[...] = _dot(u, w_ref[:, 2 * width:]) + bz_ref[...]
    v_ref[...] = val * _sigmoid(gate)


def _cfm_sample_conv_kernel(cst_ref, v_ref, cw_ref, cb_ref, c_ref, cst_out_ref):
    kw = cw_ref.shape[0]
    v = v_ref[...]
    acc = cb_ref[...] + cw_ref[kw - 1:kw, :] * v
    for k in range(kw - 1):
        acc = acc + cw_ref[k:k + 1, :] * cst_ref[k]
    c_ref[...] = acc
    for k in range(kw - 2):
        cst_out_ref[k] = cst_ref[k + 1]
    cst_out_ref[kw - 2] = v


def _cfm_sample_conv(cst, v, cw, cb, *, tok_block):
    taps, ntok, width = cst.shape
    return pl.pallas_call(
        _cfm_sample_conv_kernel,
        grid=(ntok // tok_block,),
        in_specs=[
            pl.BlockSpec((taps, tok_block, width), lambda i: (0, i, 0)),
            pl.BlockSpec((tok_block, width), lambda i: (i, 0)),
            _const_spec(cw.shape),
            _const_spec(cb.shape),
        ],
        out_specs=[
            pl.BlockSpec((tok_block, width), lambda i: (i, 0)),
            pl.BlockSpec((taps, tok_block, width), lambda i: (0, i, 0)),
        ],
        out_shape=[
            jax.ShapeDtypeStruct((ntok, width), F32),
            jax.ShapeDtypeStruct(cst.shape, F32),
        ],
        compiler_params=pltpu.CompilerParams(
            dimension_semantics=("arbitrary",),
            vmem_limit_bytes=VMEM_LIMIT_BYTES),
        name="cfm_sample_conv",
    )(cst, v, cw, cb)


def _cfm_sample_tail_kernel(c_ref, z_ref, x_ref, lng_ref, lnb_ref, wout_ref, gpost_ref, o_ref):
    c = _silu(_layer_norm(c_ref[...], lng_ref[...], lnb_ref[...]))
    o = _dot((c * _silu(z_ref[...])).astype(BF16), wout_ref[...])
    o_ref[...] = x_ref[...] + _rms(o, gpost_ref[...])


def _row(v):
    return v.reshape(1, -1)


def _pad_lanes(a):
    pad = (-a.shape[-1]) % LANES
    return jnp.pad(a, [(0, 0)] * (a.ndim - 1) + [(0, pad)])


def kernel(x_prompt, x_sample, state_ssm, state_conv_ssm, state_conv_cfm, g_pre, g_post, ssm_w_in, ssm_conv_w, ssm_conv_b, ssm_dt_bias, ssm_A_log, ssm_D, ssm_norm_g, ssm_w_out, cfm_w_in, cfm_b_in, cfm_conv_w, cfm_conv_b, cfm_ln_g, cfm_ln_b, cfm_w_out):
    d_inner = ssm_w_out.shape[1]
    conv_dim = ssm_conv_w.shape[2]
    heads = ssm_D.shape[1]
    width = cfm_w_out.shape[1]
    ntok = x_sample.shape[0]
    d = x_sample.shape[2]

    w_in = ssm_w_in[0]
    win0 = w_in.astype(BF16)
    wdt = _pad_lanes(w_in[:, d_inner + conv_dim:]).astype(BF16)
    cw0, cb0 = ssm_conv_w[0], _row(ssm_conv_b[0])
    dtb = _pad_lanes(_row(ssm_dt_bias[0]))
    alog = _pad_lanes(_row(ssm_A_log[0]))
    dexp = _row(jnp.repeat(ssm_D[0], d_inner // heads))
    ng = _row(ssm_norm_g[0])
    wout0 = ssm_w_out[0].astype(BF16)
    gpre0, gpost0 = _row(g_pre[0]), _row(g_post[0])

    w1 = cfm_w_in[0]
    w1b = w1.astype(BF16)
    wv, wg, wz1 = (w1b[:, i * width:(i + 1) * width] for i in range(3))
    bv, bg, bz = (_row(cfm_b_in[0][i * width:(i + 1) * width]) for i in range(3))
    cw1, cb1 = cfm_conv_w[0], _row(cfm_conv_b[0])
    lng, lnb = _row(cfm_ln_g[0]), _row(cfm_ln_b[0])
    wout1 = cfm_w_out[0].astype(BF16)
    gpre1, gpost1 = _row(g_pre[1]), _row(g_post[1])

    xp1, ssm_p, cssm_p = _ssd_prompt_layer(
        x_prompt, gpre0, win0, wdt, cw0, cb0, dtb, alog, dexp, ng, wout0, gpost0, tile=256)
    def col_blocks(w):
        return jnp.transpose(w.reshape(w.shape[0], width // CFM_CBLK, CFM_CBLK), (1, 0, 2))

    def mixed_blocks(g, v, z):
        g, v, z = col_blocks(g), col_blocks(v), col_blocks(z)
        lo, hi = slice(0, LANES), slice(LANES, 2 * LANES)
        return jnp.concatenate([g[..., lo], v[..., lo], g[..., hi], z[..., lo], v[..., hi], z[..., hi]], axis=2)

    def lane_blocks(a):
        return jnp.transpose(a.reshape(a.shape[0], width // LANES, LANES), (1, 0, 2))

    xp2, ccfm_p = _cfm_prompt_layer(
        xp1, gpre1,
        mixed_blocks(wg, wv, wz1), mixed_blocks(bg, bv, bz),
        lane_blocks(cw1), lane_blocks(cb1), lng, lnb, wout1, gpost1, tile=256)

    xs0 = x_sample.reshape(ntok, d)
    cst0 = jnp.transpose(state_conv_ssm[0], (1, 0, 2))
    gn = N_GROUPS * D_STATE
    sds = jax.ShapeDtypeStruct
    z_s, xs_s, b_s, c_s, xdt_t, da, cst0_new = _single_step_call(
        _ssd_sample_front_kernel,
        [sds((ntok, d_inner), F32), sds((ntok, d_inner), F32), sds((ntok, gn), F32),
         sds((ntok, gn), F32), sds((d_inner, ntok), BF16), sds((ntok, heads), F32),
         sds(cst0.shape, F32)],
        xs0, gpre0, win0, wdt, cst0, cw0, cb0, dtb, alog,
        name="ssd_sample_front")
    h0 = state_ssm[0].reshape(ntok, heads * HEAD_DIM, D_STATE)
    h_new, y_t = _ssd_sample_state(da, h0, xdt_t, b_s, c_s, tok_block=8)
    (xs1,) = _single_step_call(
        _ssd_sample_tail_kernel, [sds((ntok, d), F32)],
        y_t, xs_s, z_s, xs0, dexp, ng, wout0, gpost0, name="ssd_sample_tail")
    v_s, zc_s = _single_step_call(
        _cfm_sample_front_kernel, [sds((ntok, width), F32), sds((ntok, width), F32)],
        xs1, gpre1, w1b, bv, bg, bz, name="cfm_sample_front")
    c_conv, cst1_new = _cfm_sample_conv(
        jnp.transpose(state_conv_cfm[0], (1, 0, 2)), v_s, cw1, cb1, tok_block=16)
    (xs2,) = _single_step_call(
        _cfm_sample_tail_kernel, [sds((ntok, d), F32)],
        c_conv, zc_s, xs1, lng, lnb, wout1, gpost1, name="cfm_sample_tail")

    return (xp2,
            xs2.reshape(x_sample.shape),
            ssm_p[None],
            cssm_p[None],
            ccfm_p[None],
            h_new.reshape(state_ssm.shape),
            jnp.transpose(cst0_new, (1, 0, 2))[None],
            jnp.transpose(cst1_new, (1, 0, 2))[None])
```

```python
import functools

import jax
import jax.numpy as jnp
from jax import lax
from jax.experimental import pallas as pl
from jax.experimental.pallas import tpu as pltpu

F32 = jnp.float32
BF16 = jnp.bfloat16

EPS = 1e-6
LOG2_E = 1.4426950408889634
LANES = 128
CHUNK = 128
HEAD_DIM = 64
D_STATE = 128
N_GROUPS = 4
HEADS_PER_GROUP = 8
CFM_HALO_SLABS = 32
CFM_CBLK = 256
CFM_SLAB_GROUP = 8
VMEM_LIMIT_BYTES = 56 * 1024 * 1024


def _dot(a, b):
    return jnp.dot(a, b, preferred_element_type=F32)


def _dot_nt(a, b):
    return lax.dot_general(a, b, (((1,), (1,)), ((), ())), preferred_element_type=F32)


def _dot_tn(a, b):
    return lax.dot_general(a, b, (((0,), (0,)), ((), ())), preferred_element_type=F32)


def _rms(x, g):
    return x * lax.rsqrt(jnp.mean(x * x, axis=-1, keepdims=True) + EPS) * g


def _layer_norm(x, g, b):
    mu = jnp.mean(x, axis=-1, keepdims=True)
    xc = x - mu
    return xc * lax.rsqrt(jnp.mean(xc * xc, axis=-1, keepdims=True) + EPS) * g + b


def _sigmoid(x):
    return 1.0 / (1.0 + jnp.exp(-x))


def _silu(x):
    h = 0.5 * x
    return h + h * jnp.tanh(h)


def _softplus(x):
    return jnp.maximum(x, 0.0) + jnp.log1p(jnp.exp(-jnp.abs(x)))


def _dot_f32_rhs(l_bf16, a):
    hi = a.astype(BF16)
    r1 = a - hi.astype(F32)
    mid = r1.astype(BF16)
    lo = (r1 - mid.astype(F32)).astype(BF16)
    return _dot(l_bf16, hi) + _dot(l_bf16, mid) + _dot(l_bf16, lo)


def _dot_f32_lhs(a, r_bf16):
    hi = a.astype(BF16)
    r1 = a - hi.astype(F32)
    mid = r1.astype(BF16)
    lo = (r1 - mid.astype(F32)).astype(BF16)
    return _dot(hi, r_bf16) + _dot(mid, r_bf16) + _dot(lo, r_bf16)


def _col(a, j):
    return a[:, j:j + 1]


def _ssd_prompt_kernel(x_ref, gpre_ref, win_ref, wdt_ref, cw_ref, cb_ref,
                       dtb_ref, alog_ref, dexp_ref, ng_ref, wout_ref, gpost_ref,
                       y_ref, hout_ref, cout_ref,
                       ht_s, tail_s, xs_s, xdt_s, b_s, c_s, z_s, a_s, yb_s):
    t = pl.program_id(1)
    nt = pl.num_programs(1)
    tile = x_ref.shape[0]
    n_chunks = tile // CHUNK
    d_inner = xs_s.shape[1]
    gn = N_GROUPS * D_STATE
    kw = cw_ref.shape[0]
    halo = 8

    @pl.when(t == 0)
    def _():
        ht_s[...] = jnp.zeros_like(ht_s)
        tail_s[...] = jnp.zeros_like(tail_s)

    x = x_ref[...]
    u = _rms(x, gpre_ref[...]).astype(BF16)
    z_s[...] = _dot(u, win_ref[:, 0:d_inner])
    xbc = _dot(u, win_ref[:, d_inner:d_inner + cw_ref.shape[1]])
    dt_raw = _dot(u, wdt_ref[...])

    conv_dim = xbc.shape[1]
    rows8 = jnp.concatenate([tail_s[...].reshape(1, halo, conv_dim),
                             xbc.reshape(tile // halo, halo, conv_dim)], axis=0)
    sub = lax.broadcasted_iota(jnp.int32, (tile // halo, halo, conv_dim), 1)
    acc = cb_ref[...] + cw_ref[kw - 1:kw, :] * xbc
    for sh in range(1, kw):
        rot = jnp.concatenate([rows8[:, halo - sh:, :], rows8[:, :halo - sh, :]], axis=1)
        win = jnp.where(sub < sh, rot[:-1], rot[1:]).reshape(tile, conv_dim)
        acc = acc + cw_ref[kw - 1 - sh:kw - sh, :] * win
    last8 = xbc[tile - halo:, :]
    cout_ref[...] = last8[halo - (kw - 1):, :]
    tail_s[...] = last8
    act = _silu(acc)
    xs = act[:, :d_inner]
    xs_s[...] = xs
    b_s[...] = act[:, d_inner:d_inner + gn].astype(BF16)
    c_s[...] = act[:, d_inner + gn:].astype(BF16)
    dt = _softplus(dt_raw + dtb_ref[...])
    a_s[...] = dt * (-jnp.exp(alog_ref[...]) * LOG2_E)
    hrow = lax.broadcasted_iota(jnp.int32, (LANES, d_inner), 0)
    ccol = lax.broadcasted_iota(jnp.int32, (LANES, d_inner), 1)
    expand = ((ccol >> 6) == hrow).astype(BF16)
    xdt_s[...] = xs * _dot_f32_lhs(dt, expand)

    row = lax.broadcasted_iota(jnp.int32, (CHUNK, CHUNK), 0)
    lane = lax.broadcasted_iota(jnp.int32, (CHUNK, CHUNK), 1)
    causal = row >= lane
    tri = causal.astype(BF16)
    lo_half = lane < HEAD_DIM
    lo_half_row = lo_half[0:1, :]
    dexp = dexp_ref[...]

    def chunk_body(c, carry):
        r0 = pl.multiple_of(c * CHUNK, CHUNK)
        rows = pl.ds(r0, CHUNK)
        a_c = a_s[rows, :]
        a_cum = _dot_f32_rhs(tri, a_c)
        a_cum_t = a_cum.T
        a_last = a_cum[CHUNK - 1:CHUNK, :]
        for g in range(N_GROUPS):
            gl = slice(g * D_STATE, (g + 1) * D_STATE)
            cg = c_s[rows, gl]
            bg = b_s[rows, gl]
            cb = _dot_nt(cg, bg)
            ht_g = ht_s[g]
            yoff_g = _dot(cg, ht_g.astype(BF16))
            xw_parts = []
            dec_parts = []
            for j in range(HEADS_PER_GROUP // 2):
                h0 = g * HEADS_PER_GROUP + 2 * j
                h1 = h0 + 1
                pl_ = slice((g * 4 + j) * LANES, (g * 4 + j + 1) * LANES)
                xdt_p = xdt_s[rows, pl_]
                acol0 = jnp.broadcast_to(_col(a_cum, h0), (CHUNK, CHUNK))
                acol1 = jnp.broadcast_to(_col(a_cum, h1), (CHUNK, CHUNK))

                def scores(h, acol):
                    diff = acol - a_cum_t[h:h + 1, :]
                    dec = jnp.exp2(jnp.where(causal, diff, -jnp.inf))
                    return (cb * dec).astype(BF16)

                x0 = jnp.where(lo_half, xdt_p, 0.0).astype(BF16)
                x1 = jnp.where(lo_half, 0.0, xdt_p).astype(BF16)
                y_diag = _dot(scores(h0, acol0), x0) + _dot(scores(h1, acol1), x1)
                a_pair = jnp.where(lo_half, acol0, acol1)
                last_pair = jnp.where(lo_half_row, a_last[:, h0:h0 + 1], a_last[:, h1:h1 + 1])
                y_pair = (y_diag + jnp.exp2(a_pair) * yoff_g[:, j * LANES:(j + 1) * LANES]
                          + dexp[:, pl_] * xs_s[rows, pl_])
                yb_s[rows, pl_] = y_pair
                xw_parts.append((xdt_p * jnp.exp2(last_pair - a_pair)).astype(BF16))
                dec_parts.append(jnp.exp2(last_pair))
            xw_g = jnp.concatenate(xw_parts, axis=1)
            dec_g = jnp.concatenate(dec_parts, axis=1)
            ht_s[g] = ht_g * dec_g + _dot_tn(bg, xw_g)
        return carry

    lax.fori_loop(0, n_chunks, chunk_body, 0, unroll=True)

    yg = yb_s[...] * _silu(z_s[...])
    yn = _rms(yg, ng_ref[...]).astype(BF16)
    o = _dot(yn, wout_ref[...])
    y_ref[...] = x + _rms(o, gpost_ref[...])

    @pl.when(t == nt - 1)
    def _():
        for g in range(N_GROUPS):
            hg = ht_s[g].T
            for r in range(HEADS_PER_GROUP):
                hout_ref[g * HEADS_PER_GROUP + r] = hg[r * HEAD_DIM:(r + 1) * HEAD_DIM, :]


def _const_spec(shape):
    nd = len(shape)
    return pl.BlockSpec(shape, lambda *_: (0,) * nd, pipeline_mode=pl.Buffered(1))


def _ssd_prompt_layer(x, gpre, win, wdt, cw, cb, dtb, alog, dexp, ng, wout, gpost, *, tile):
    bsz, seq, d = x.shape
    d_inner = wout.shape[0]
    conv_dim = cw.shape[1]
    gn = N_GROUPS * D_STATE
    heads = d_inner // HEAD_DIM
    kw = cw.shape[0]
    consts = (gpre, win, wdt, cw, cb, dtb, alog, dexp, ng, wout, gpost)
    return pl.pallas_call(
        _ssd_prompt_kernel,
        grid=(bsz, seq // tile),
        in_specs=[pl.BlockSpec((None, tile, d), lambda b, t: (b, t, 0))]
        + [_const_spec(c.shape) for c in consts],
        out_specs=[
            pl.BlockSpec((None, tile, d), lambda b, t: (b, t, 0)),
            pl.BlockSpec((None, heads, HEAD_DIM, D_STATE), lambda b, t: (b, 0, 0, 0)),
            pl.BlockSpec((None, kw - 1, conv_dim), lambda b, t: (b, 0, 0)),
        ],
        out_shape=[
            jax.ShapeDtypeStruct((bsz, seq, d), F32),
            jax.ShapeDtypeStruct((bsz, heads, HEAD_DIM, D_STATE), F32),
            jax.ShapeDtypeStruct((bsz, kw - 1, conv_dim), F32),
        ],
        scratch_shapes=[
            pltpu.VMEM((N_GROUPS, D_STATE, HEADS_PER_GROUP * HEAD_DIM), F32),
            pltpu.VMEM((8, conv_dim), F32),
            pltpu.VMEM((tile, d_inner), F32),
            pltpu.VMEM((tile, d_inner), F32),
            pltpu.VMEM((tile, gn), BF16),
            pltpu.VMEM((tile, gn), BF16),
            pltpu.VMEM((tile, d_inner), F32),
            pltpu.VMEM((tile, LANES), F32),
            pltpu.VMEM((tile, d_inner), F32),
        ],
        compiler_params=pltpu.CompilerParams(
            dimension_semantics=("arbitrary", "arbitrary"),
            vmem_limit_bytes=VMEM_LIMIT_BYTES),
        name="ssd_prompt_layer",
    )(x, *consts)


def _cfm_prompt_kernel(xa_ref, xb_ref, gpre_ref, win_ref, bin_ref,
                       cw_ref, cb_ref, lng_ref, lnb_ref, wout_ref, gpost_ref,
                       y_ref, cout_ref, up_s, ext_s, z_s, acc_s, taps_s, *, tiles_per_seq):
    s = pl.program_id(0)
    n_tiles = pl.num_programs(0) - 1
    tile = xa_ref.shape[0]
    n_cblk = win_ref.shape[0]
    cblk = win_ref.shape[2] // 3
    lanes_per_cblk = cblk // LANES
    kw = cw_ref.shape[1]
    nslab = tile // 8
    nh = kw - 1
    hb = CFM_HALO_SLABS * 8
    nxt = s % 2
    cur = 1 - nxt
    first_of_seq = (jnp.minimum(s, n_tiles - 1) % tiles_per_seq) == 0

    @pl.when(s == 0)
    def _():
        ext_s[...] = jnp.zeros_like(ext_s)
        z_s[...] = jnp.zeros_like(z_s)
        for lb in range(n_cblk * lanes_per_cblk):
            for k in range(kw):
                taps_s[lb, k * 8:(k + 1) * 8, :] = jnp.broadcast_to(cw_ref[lb, k:k + 1, :], (8, LANES))

    ri = lax.broadcasted_iota(jnp.int32, (tile, tile), 0)
    ci = lax.broadcasted_iota(jnp.int32, (tile, tile), 1)
    to_perm = (ci == (ri & 7) * nslab + (ri >> 3)).astype(BF16)
    to_nat = (ri == (ci & 7) * nslab + (ci >> 3)).astype(BF16)
    u = _rms(xa_ref[...], gpre_ref[...]).astype(BF16)
    up_s[0:tile, :] = _dot(to_perm, u).astype(BF16)
    sub = lax.broadcasted_iota(jnp.int32, (nh * 8, LANES), 0) & 7

    def block_body(c, carry):
        lbs = [c * lanes_per_cblk + e for e in range(lanes_per_cblk)]
        prev_tails = [ext_s[cur, lb, hb + (nslab - nh) * 8:hb + tile, :] for lb in lbs]

        def conv_unit(j0, lb):
            accs = [jnp.broadcast_to(cb_ref[lb], (8, LANES))] * CFM_SLAB_GROUP
            for m in range(j0, j0 + CFM_SLAB_GROUP + kw - 1):
                e_m = ext_s[cur, lb, pl.ds((CFM_HALO_SLABS - nh + m) * 8, 8), :]
                for jj in range(CFM_SLAB_GROUP):
                    k = m - j0 - jj
                    if 0 <= k < kw:
                        accs[jj] = accs[jj] + taps_s[lb, pl.ds(k * 8, 8), :] * e_m
            for jj in range(CFM_SLAB_GROUP):
                acc_s[lb, pl.ds((j0 + jj) * 8, 8), :] = accs[jj]
            return accs[-1]

        def gated_input(after):
            up_s[tile:tile + 16, 0:LANES] = jnp.concatenate([after, after], axis=0).astype(BF16)
            return up_s[pl.ds(pl.multiple_of(jnp.minimum(s, 0) * 16, 16), tile), :]

        units = [(j0, lb) for j0 in range(0, nslab, CFM_SLAB_GROUP) for lb in lbs]
        proj = _dot(gated_input(conv_unit(*units[0])), win_ref[c]) + bin_ref[c]
        part = [proj[:, i * LANES:(i + 1) * LANES] for i in range(6)]
        z_s[nxt, c] = jnp.concatenate([part[3], part[5]], axis=1)
        for unit in units[1:]:
            conv_unit(*unit)
        v = jnp.concatenate([part[1] * _sigmoid(part[0]), part[4] * _sigmoid(part[2])], axis=1)
        cout_ref[c] = v.reshape(nslab, 8, cblk)[nslab - nh:, 7, :]
        for e in range(lanes_per_cblk):
            lb = c * lanes_per_cblk + e
            v_e = v[:, e * LANES:(e + 1) * LANES]
            prev_tail = jnp.where(first_of_seq, 0.0, prev_tails[e])
            mixed = jnp.where(sub == 7, prev_tail, v_e[(nslab - nh) * 8:, :]).reshape(nh, 8, LANES)
            halo = jnp.concatenate([mixed[:, 7:8, :], mixed[:, 0:7, :]], axis=1)
            ext_s[nxt, lb, hb - nh * 8:hb, :] = halo.reshape(nh * 8, LANES)
            ext_s[nxt, lb, hb:hb + tile, :] = v_e
        return carry

    lax.fori_loop(0, n_cblk, block_body, 0)

    acc = jnp.concatenate([acc_s[lb] for lb in range(n_cblk * lanes_per_cblk)], axis=1)
    xc = acc - jnp.mean(acc, axis=-1, keepdims=True)
    rstd = lax.rsqrt(jnp.mean(xc * xc, axis=-1, keepdims=True) + EPS)
    o = jnp.zeros(y_ref.shape, F32)
    for c in range(n_cblk):
        cols = slice(c * cblk, (c + 1) * cblk)
        cact = _silu(xc[:, cols] * rstd * lng_ref[:, cols] + lnb_ref[:, cols])
        m_p = (cact * _silu(z_s[cur, c])).astype(BF16)
        m_c = _dot(to_nat, m_p).astype(BF16)
        o = o + _dot(m_c, wout_ref[cols, :])
    y_ref[...] = xb_ref[...] + _rms(o, gpost_ref[...])


def _cfm_prompt_layer(x, gpre, win, bin_, cw, cb, lng, lnb, wout, gpost, *, tile):
    bsz, seq, d = x.shape
    n_cblk = win.shape[0]
    cblk = win.shape[2] // 3
    width = n_cblk * cblk
    n_lblk, kw, _ = cw.shape
    tiles_per_seq = seq // tile
    n_tiles = bsz * tiles_per_seq
    assert kw - 1 <= CFM_HALO_SLABS <= tile // 8 and seq % tile == 0 and n_lblk * LANES == width
    assert cblk == 2 * LANES
    consts = (gpre, win, bin_, cw, cb, lng, lnb, wout, gpost)

    def tile_a(s):
        ta = jnp.minimum(s, n_tiles - 1)
        return ta // tiles_per_seq, ta % tiles_per_seq

    def tile_b(s):
        tb = jnp.maximum(s - 1, 0)
        return tb // tiles_per_seq, tb % tiles_per_seq

    y, cout = pl.pallas_call(
        functools.partial(_cfm_prompt_kernel, tiles_per_seq=tiles_per_seq),
        grid=(n_tiles + 1,),
        in_specs=[pl.BlockSpec((None, tile, d), lambda s: (*tile_a(s), 0)),
                  pl.BlockSpec((None, tile, d), lambda s: (*tile_b(s), 0))]
        + [_const_spec(c.shape) for c in consts],
        out_specs=[
            pl.BlockSpec((None, tile, d), lambda s: (*tile_b(s), 0)),
            pl.BlockSpec((None, n_cblk, kw - 1, cblk), lambda s: (tile_a(s)[0], 0, 0, 0)),
        ],
        out_shape=[
            jax.ShapeDtypeStruct((bsz, seq, d), F32),
            jax.ShapeDtypeStruct((bsz, n_cblk, kw - 1, cblk), F32),
        ],
        scratch_shapes=[
            pltpu.VMEM((tile + 16, d), BF16),
            pltpu.VMEM((2, n_lblk, CFM_HALO_SLABS * 8 + tile, LANES), F32),
            pltpu.VMEM((2, n_cblk, tile, cblk), F32),
            pltpu.VMEM((n_lblk, tile, LANES), F32),
            pltpu.VMEM((n_lblk, kw * 8, LANES), F32),
        ],
        compiler_params=pltpu.CompilerParams(
            dimension_semantics=("arbitrary",),
            vmem_limit_bytes=VMEM_LIMIT_BYTES),
        name="cfm_prompt_layer",
    )(x, x, *consts)
    return y, jnp.transpose(cout, (0, 2, 1, 3)).reshape(bsz, kw - 1, width)


def _single_step_call(kernel, out_shapes, *args, name):
    return pl.pallas_call(
        kernel,
        grid=(1,),
        in_specs=[_const_spec(a.shape) for a in args],
        out_specs=[_const_spec(s.shape) for s in out_shapes],
        out_shape=out_shapes,
        compiler_params=pltpu.CompilerParams(
            dimension_semantics=("arbitrary",),
            vmem_limit_bytes=VMEM_LIMIT_BYTES),
        name=name,
    )(*args)


def _ssd_sample_front_kernel(x_ref, gpre_ref, win_ref, wdt_ref, cst_ref, cw_ref, cb_ref,
                             dtb_ref, alog_ref,
                             z_ref, xs_ref, b_ref, c_ref, xdt_t_ref, da_ref, cst_out_ref):
    d_inner = xs_ref.shape[1]
    gn = N_GROUPS * D_STATE
    kw = cw_ref.shape[0]
    u = _rms(x_ref[...], gpre_ref[...]).astype(BF16)
    z_ref[...] = _dot(u, win_ref[:, 0:d_inner])
    xbc = _dot(u, win_ref[:, d_inner:d_inner + cw_ref.shape[1]])
    dt_raw = _dot(u, wdt_ref[...])

    acc = cb_ref[...] + cw_ref[kw - 1:kw, :] * xbc
    for k in range(kw - 1):
        acc = acc + cw_ref[k:k + 1, :] * cst_ref[k]
    for k in range(kw - 2):
        cst_out_ref[k] = cst_ref[k + 1]
    cst_out_ref[kw - 2] = xbc
    act = _silu(acc)
    xs = act[:, :d_inner]
    xs_ref[...] = xs
    b_ref[...] = act[:, d_inner:d_inner + gn]
    c_ref[...] = act[:, d_inner + gn:]

    dt = _softplus(dt_raw + dtb_ref[...])
    da_ref[...] = jnp.exp(dt * (-jnp.exp(alog_ref[...])))[:, :da_ref.shape[1]]
    hrow = lax.broadcasted_iota(jnp.int32, (LANES, d_inner), 0)
    ccol = lax.broadcasted_iota(jnp.int32, (LANES, d_inner), 1)
    expand = ((ccol >> 6) == hrow).astype(BF16)
    xdt_t_ref[...] = (xs * _dot_f32_lhs(dt, expand)).T.astype(BF16)


def _ssd_sample_state_kernel(da_ref, h_ref, xdt_t_ref, ball_ref, c_ref, hn_ref, yt_ref):
    step = pl.program_id(0)
    tok_block = h_ref.shape[0]
    ntok = ball_ref.shape[0]
    rpg = HEADS_PER_GROUP * HEAD_DIM
    tok_row = lax.broadcasted_iota(jnp.int32, (ntok, D_STATE), 0)
    tok_lane = lax.broadcasted_iota(jnp.int32, (rpg, ntok), 1)
    blk_row = lax.broadcasted_iota(jnp.int32, (tok_block, D_STATE), 0)

    @pl.when(step == 0)
    def _():
        yt_ref[...] = jnp.zeros_like(yt_ref)

    def token_body(j, carry):
        tok = step * tok_block + j
        for g in range(N_GROUPS):
            gl = slice(g * D_STATE, (g + 1) * D_STATE)
            gr = slice(g * rpg, (g + 1) * rpg)
            b_tok = jnp.where(tok_row == tok, ball_ref[:, gl], 0.0).astype(BF16)
            outer = _dot(xdt_t_ref[gr, :], b_tok)
            scaled = [h_ref[j, (g * HEADS_PER_GROUP + r) * HEAD_DIM:(g * HEADS_PER_GROUP + r + 1) * HEAD_DIM, :]
                      * da_ref[tok, g * HEADS_PER_GROUP + r] for r in range(HEADS_PER_GROUP)]
            hn_g = jnp.concatenate(scaled, axis=0) + outer
            hn_ref[j, gr, :] = hn_g
            c_tok = jnp.sum(jnp.where(blk_row == j, c_ref[:, gl], 0.0), axis=0, keepdims=True)
            ycol = jnp.sum(hn_g * c_tok, axis=1, keepdims=True)
            yt_ref[gr, :] = jnp.where(tok_lane == tok, ycol, yt_ref[gr, :])
        return carry

    lax.fori_loop(0, tok_block, token_body, 0)


def _ssd_sample_state(da, h0, xdt_t, b_all, c_all, *, tok_block):
    ntok, rows, n = h0.shape
    return pl.pallas_call(
        _ssd_sample_state_kernel,
        grid=(ntok // tok_block,),
        in_specs=[
            pl.BlockSpec(memory_space=pltpu.SMEM),
            pl.BlockSpec((tok_block, rows, n), lambda i: (i, 0, 0)),
            _const_spec(xdt_t.shape),
            _const_spec(b_all.shape),
            pl.BlockSpec((tok_block, c_all.shape[1]), lambda i: (i, 0)),
        ],
        out_specs=[
            pl.BlockSpec((tok_block, rows, n), lambda i: (i, 0, 0)),
            pl.BlockSpec((rows, ntok), lambda i: (0, 0)),
        ],
        out_shape=[
            jax.ShapeDtypeStruct(h0.shape, F32),
            jax.ShapeDtypeStruct((rows, ntok), F32),
        ],
        compiler_params=pltpu.CompilerParams(
            dimension_semantics=("arbitrary",),
            vmem_limit_bytes=VMEM_LIMIT_BYTES),
        name="ssd_sample_state",
    )(da, h0, xdt_t, b_all, c_all)


def _ssd_sample_tail_kernel(yt_ref, xs_ref, z_ref, x_ref, dexp_ref, ng_ref, wout_ref, gpost_ref, o_ref):
    y = yt_ref[...].T + dexp_ref[...] * xs_ref[...]
    yg = y * _silu(z_ref[...])
    yn = _rms(yg, ng_ref[...]).astype(BF16)
    o = _dot(yn, wout_ref[...])
    o_ref[...] = x_ref[...] + _rms(o, gpost_ref[...])


def _cfm_sample_front_kernel(x_ref, gpre_ref, w_ref, bv_ref, bg_ref, bz_ref,
                             v_ref, z_ref):
    u = _rms(x_ref[...], gpre_ref[...]).astype(BF16)
    width = v_ref.shape[1]
    val = _dot(u, w_ref[:, 0:width]) + bv_ref[...]
    gate = _dot(u, w_ref[:, width:2 * width]) + bg_ref[...]
    z_ref[...] = _dot(u, w_ref[:, 2 * width:]) + bz_ref[...]
    v_ref[...] = val * _sigmoid(gate)


def _cfm_sample_conv_kernel(cst_ref, v_ref, cw_ref, cb_ref, c_ref, cst_out_ref):
    kw = cw_ref.shape[0]
    v = v_ref[...]
    acc = cb_ref[...] + cw_ref[kw - 1:kw, :] * v
    for k in range(kw - 1):
        acc = acc + cw_ref[k:k + 1, :] * cst_ref[k]
    c_ref[...] = acc
    for k in range(kw - 2):
        cst_out_ref[k] = cst_ref[k + 1]
    cst_out_ref[kw - 2] = v


def _cfm_sample_conv(cst, v, cw, cb, *, tok_block):
    taps, ntok, width = cst.shape
    return pl.pallas_call(
        _cfm_sample_conv_kernel,
        grid=(ntok // tok_block,),
        in_specs=[
            pl.BlockSpec((taps, tok_block, width), lambda i: (0, i, 0)),
            pl.BlockSpec((tok_block, width), lambda i: (i, 0)),
            _const_spec(cw.shape),
            _const_spec(cb.shape),
        ],
        out_specs=[
            pl.BlockSpec((tok_block, width), lambda i: (i, 0)),
            pl.BlockSpec((taps, tok_block, width), lambda i: (0, i, 0)),
        ],
        out_shape=[
            jax.ShapeDtypeStruct((ntok, width), F32),
            jax.ShapeDtypeStruct(cst.shape, F32),
        ],
        compiler_params=pltpu.CompilerParams(
            dimension_semantics=("arbitrary",),
            vmem_limit_bytes=VMEM_LIMIT_BYTES),
        name="cfm_sample_conv",
    )(cst, v, cw, cb)


def _cfm_sample_tail_kernel(c_ref, z_ref, x_ref, lng_ref, lnb_ref, wout_ref, gpost_ref, o_ref):
    c = _silu(_layer_norm(c_ref[...], lng_ref[...], lnb_ref[...]))
    o = _dot((c * _silu(z_ref[...])).astype(BF16), wout_ref[...])
    o_ref[...] = x_ref[...] + _rms(o, gpost_ref[...])


def _row(v):
    return v.reshape(1, -1)


def _pad_lanes(a):
    pad = (-a.shape[-1]) % LANES
    return jnp.pad(a, [(0, 0)] * (a.ndim - 1) + [(0, pad)])


def kernel(x_prompt, x_sample, state_ssm, state_conv_ssm, state_conv_cfm, g_pre, g_post, ssm_w_in, ssm_conv_w, ssm_conv_b, ssm_dt_bias, ssm_A_log, ssm_D, ssm_norm_g, ssm_w_out, cfm_w_in, cfm_b_in, cfm_conv_w, cfm_conv_b, cfm_ln_g, cfm_ln_b, cfm_w_out):
    d_inner = ssm_w_out.shape[1]
    conv_dim = ssm_conv_w.shape[2]
    heads = ssm_D.shape[1]
    width = cfm_w_out.shape[1]
    ntok = x_sample.shape[0]
    d = x_sample.shape[2]

    w_in = ssm_w_in[0]
    win0 = w_in.astype(BF16)
    wdt = _pad_lanes(w_in[:, d_inner + conv_dim:]).astype(BF16)
    cw0, cb0 = ssm_conv_w[0], _row(ssm_conv_b[0])
    dtb = _pad_lanes(_row(ssm_dt_bias[0]))
    alog = _pad_lanes(_row(ssm_A_log[0]))
    dexp = _row(jnp.repeat(ssm_D[0], d_inner // heads))
    ng = _row(ssm_norm_g[0])
    wout0 = ssm_w_out[0].astype(BF16)
    gpre0, gpost0 = _row(g_pre[0]), _row(g_post[0])

    w1 = cfm_w_in[0]
    w1b = w1.astype(BF16)
    wv, wg, wz1 = (w1b[:, i * width:(i + 1) * width] for i in range(3))
    bv, bg, bz = (_row(cfm_b_in[0][i * width:(i + 1) * width]) for i in range(3))
    cw1, cb1 = cfm_conv_w[0], _row(cfm_conv_b[0])
    lng, lnb = _row(cfm_ln_g[0]), _row(cfm_ln_b[0])
    wout1 = cfm_w_out[0].astype(BF16)
    gpre1, gpost1 = _row(g_pre[1]), _row(g_post[1])

    xp1, ssm_p, cssm_p = _ssd_prompt_layer(
        x_prompt, gpre0, win0, wdt, cw0, cb0, dtb, alog, dexp, ng, wout0, gpost0, tile=256)
    def col_blocks(w):
        return jnp.transpose(w.reshape(w.shape[0], width // CFM_CBLK, CFM_CBLK), (1, 0, 2))

    def mixed_blocks(g, v, z):
        g, v, z = col_blocks(g), col_blocks(v), col_blocks(z)
        lo, hi = slice(0, LANES), slice(LANES, 2 * LANES)
        return jnp.concatenate([g[..., lo], v[..., lo], g[..., hi], z[..., lo], v[..., hi], z[..., hi]], axis=2)

    def lane_blocks(a):
        return jnp.transpose(a.reshape(a.shape[0], width // LANES, LANES), (1, 0, 2))

    xp2, ccfm_p = _cfm_prompt_layer(
        xp1, gpre1,
        mixed_blocks(wg, wv, wz1), mixed_blocks(bg, bv, bz),
        lane_blocks(cw1), lane_blocks(cb1), lng, lnb, wout1, gpost1, tile=256)

    xs0 = x_sample.reshape(ntok, d)
    cst0 = jnp.transpose(state_conv_ssm[0], (1, 0, 2))
    gn = N_GROUPS * D_STATE
    sds = jax.ShapeDtypeStruct
    z_s, xs_s, b_s, c_s, xdt_t, da, cst0_new = _single_step_call(
        _ssd_sample_front_kernel,
        [sds((ntok, d_inner), F32), sds((ntok, d_inner), F32), sds((ntok, gn), F32),
         sds((ntok, gn), F32), sds((d_inner, ntok), BF16), sds((ntok, heads), F32),
         sds(cst0.shape, F32)],
        xs0, gpre0, win0, wdt, cst0, cw0, cb0, dtb, alog,
        name="ssd_sample_front")
    h0 = state_ssm[0].reshape(ntok, heads * HEAD_DIM, D_STATE)
    h_new, y_t = _ssd_sample_state(da, h0, xdt_t, b_s, c_s, tok_block=8)
    (xs1,) = _single_step_call(
        _ssd_sample_tail_kernel, [sds((ntok, d), F32)],
        y_t, xs_s, z_s, xs0, dexp, ng, wout0, gpost0, name="ssd_sample_tail")
    v_s, zc_s = _single_step_call(
        _cfm_sample_front_kernel, [sds((ntok, width), F32), sds((ntok, width), F32)],
        xs1, gpre1, w1b, bv, bg, bz, name="cfm_sample_front")
    c_conv, cst1_new = _cfm_sample_conv(
        jnp.transpose(state_conv_cfm[0], (1, 0, 2)), v_s, cw1, cb1, tok_block=16)
    (xs2,) = _single_step_call(
        _cfm_sample_tail_kernel, [sds((ntok, d), F32)],
        c_conv, zc_s, xs1, lng, lnb, wout1, gpost1, name="cfm_sample_tail")

    return (xp2,
            xs2.reshape(x_sample.shape),
            ssm_p[None],
            cssm_p[None],
            ccfm_p[None],
            h_new.reshape(state_ssm.shape),
            jnp.transpose(cst0_new, (1, 0, 2))[None],
            jnp.transpose(cst1_new, (1, 0, 2))[None])
```

```python
import functools

import jax
import jax.numpy as jnp
from jax import lax
from jax.experimental import pallas as pl
from jax.experimental.pallas import tpu as pltpu

F32 = jnp.float32
BF16 = jnp.bfloat16

EPS = 1e-6
LOG2_E = 1.4426950408889634
LANES = 128
CHUNK = 128
HEAD_DIM = 64
D_STATE = 128
N_GROUPS = 4
HEADS_PER_GROUP = 8
CFM_HALO_SLABS = 32
CFM_CBLK = 256
CFM_SLAB_GROUP = 8
VMEM_LIMIT_BYTES = 56 * 1024 * 1024


def _dot(a, b):
    return jnp.dot(a, b, preferred_element_type=F32)


def _dot_nt(a, b):
    return lax.dot_general(a, b, (((1,), (1,)), ((), ())), preferred_element_type=F32)


def _dot_tn(a, b):
    return lax.dot_general(a, b, (((0,), (0,)), ((), ())), preferred_element_type=F32)


def _rms(x, g):
    return x * lax.rsqrt(jnp.mean(x * x, axis=-1, keepdims=True) + EPS) * g


def _layer_norm(x, g, b):
    mu = jnp.mean(x, axis=-1, keepdims=True)
    xc = x - mu
    return xc * lax.rsqrt(jnp.mean(xc * xc, axis=-1, keepdims=True) + EPS) * g + b


def _sigmoid(x):
    return 1.0 / (1.0 + jnp.exp(-x))


def _silu(x):
    h = 0.5 * x
    return h + h * jnp.tanh(h)


def _softplus(x):
    return jnp.maximum(x, 0.0) + jnp.log1p(jnp.exp(-jnp.abs(x)))


def _dot_f32_rhs(l_bf16, a):
    hi = a.astype(BF16)
    r1 = a - hi.astype(F32)
    mid = r1.astype(BF16)
    lo = (r1 - mid.astype(F32)).astype(BF16)
    return _dot(l_bf16, hi) + _dot(l_bf16, mid) + _dot(l_bf16, lo)


def _dot_f32_lhs(a, r_bf16):
    hi = a.astype(BF16)
    r1 = a - hi.astype(F32)
    mid = r1.astype(BF16)
    lo = (r1 - mid.astype(F32)).astype(BF16)
    return _dot(hi, r_bf16) + _dot(mid, r_bf16) + _dot(lo, r_bf16)


def _col(a, j):
    return a[:, j:j + 1]


def _ssd_prompt_kernel(x_ref, gpre_ref, win_ref, wdt_ref, cw_ref, cb_ref,
                       dtb_ref, alog_ref, dexp_ref, ng_ref, wout_ref, gpost_ref,
                       y_ref, hout_ref, cout_ref,
                       ht_s, tail_s, xs_s, xdt_s, b_s, c_s, z_s, a_s, yb_s):
    t = pl.program_id(1)
    nt = pl.num_programs(1)
    tile = x_ref.shape[0]
    n_chunks = tile // CHUNK
    d_inner = xs_s.shape[1]
    gn = N_GROUPS * D_STATE
    kw = cw_ref.shape[0]
    halo = 8

    @pl.when(t == 0)
    def _():
        ht_s[...] = jnp.zeros_like(ht_s)
        tail_s[...] = jnp.zeros_like(tail_s)

    x = x_ref[...]
    u = _rms(x, gpre_ref[...]).astype(BF16)
    z_s[...] = _dot(u, win_ref[:, 0:d_inner])
    xbc = _dot(u, win_ref[:, d_inner:d_inner + cw_ref.shape[1]])
    dt_raw = _dot(u, wdt_ref[...])

    conv_dim = xbc.shape[1]
    rows8 = jnp.concatenate([tail_s[...].reshape(1, halo, conv_dim),
                             xbc.reshape(tile // halo, halo, conv_dim)], axis=0)
    sub = lax.broadcasted_iota(jnp.int32, (tile // halo, halo, conv_dim), 1)
    acc = cb_ref[...] + cw_ref[kw - 1:kw, :] * xbc
    for sh in range(1, kw):
        rot = jnp.concatenate([rows8[:, halo - sh:, :], rows8[:, :halo - sh, :]], axis=1)
        win = jnp.where(sub < sh, rot[:-1], rot[1:]).reshape(tile, conv_dim)
        acc = acc + cw_ref[kw - 1 - sh:kw - sh, :] * win
    last8 = xbc[tile - halo:, :]
    cout_ref[...] = last8[halo - (kw - 1):, :]
    tail_s[...] = last8
    act = _silu(acc)
    xs = act[:, :d_inner]
    xs_s[...] = xs
    b_s[...] = act[:, d_inner:d_inner + gn].astype(BF16)
    c_s[...] = act[:, d_inner + gn:].astype(BF16)
    dt = _softplus(dt_raw + dtb_ref[...])
    a_s[...] = dt * (-jnp.exp(alog_ref[...]) * LOG2_E)
    hrow = lax.broadcasted_iota(jnp.int32, (2 * LANES, d_inner), 0) & (LANES - 1)
    ccol = lax.broadcasted_iota(jnp.int32, (2 * LANES, d_inner), 1)
    expand2 = ((ccol >> 6) == hrow).astype(BF16)
    dt_hi = dt.astype(BF16)
    dt_lo = (dt - dt_hi.astype(F32)).astype(BF16)
    xdt_s[...] = xs * _dot(jnp.concatenate([dt_hi, dt_lo], axis=1), expand2)

    row = lax.broadcasted_iota(jnp.int32, (CHUNK, CHUNK), 0)
    lane = lax.broadcasted_iota(jnp.int32, (CHUNK, CHUNK), 1)
    causal = row >= lane
    tri = causal.astype(BF16)
    lo_half = lane < HEAD_DIM
    lo_half_row = lo_half[0:1, :]
    dexp = dexp_ref[...]

    def chunk_body(c, carry):
        r0 = pl.multiple_of(c * CHUNK, CHUNK)
        rows = pl.ds(r0, CHUNK)
        a_c = a_s[rows, :]
        a_cum = _dot_f32_rhs(tri, a_c)
        a_cum_t = a_cum.T
        a_last = a_cum[CHUNK - 1:CHUNK, :]
        for g in range(N_GROUPS):
            gl = slice(g * D_STATE, (g + 1) * D_STATE)
            cg = c_s[rows, gl]
            bg = b_s[rows, gl]
            cb = _dot_nt(cg, bg)
            ht_g = ht_s[g]
            yoff_g = _dot(cg, ht_g.astype(BF16))
            xw_parts = []
            dec_parts = []
            for j in range(HEADS_PER_GROUP // 2):
                h0 = g * HEADS_PER_GROUP + 2 * j
                h1 = h0 + 1
                pl_ = slice((g * 4 + j) * LANES, (g * 4 + j + 1) * LANES)
                xdt_p = xdt_s[rows, pl_]
                acol0 = jnp.broadcast_to(_col(a_cum, h0), (CHUNK, CHUNK))
                acol1 = jnp.broadcast_to(_col(a_cum, h1), (CHUNK, CHUNK))

                def scores(h, acol):
                    diff = acol - a_cum_t[h:h + 1, :]
                    dec = jnp.exp2(jnp.where(causal, diff, -jnp.inf))
                    return (cb * dec).astype(BF16)

                x0 = jnp.where(lo_half, xdt_p, 0.0).astype(BF16)
                x1 = jnp.where(lo_half, 0.0, xdt_p).astype(BF16)
                y_diag = _dot(scores(h0, acol0), x0) + _dot(scores(h1, acol1), x1)
                a_pair = jnp.where(lo_half, acol0, acol1)
                last_pair = jnp.where(lo_half_row, a_last[:, h0:h0 + 1], a_last[:, h1:h1 + 1])
                y_pair = (y_diag + jnp.exp2(a_pair) * yoff_g[:, j * LANES:(j + 1) * LANES]
                          + dexp[:, pl_] * xs_s[rows, pl_])
                yb_s[rows, pl_] = y_pair
                xw_parts.append((xdt_p * jnp.exp2(last_pair - a_pair)).astype(BF16))
                dec_parts.append(jnp.exp2(last_pair))
            xw_g = jnp.concatenate(xw_parts, axis=1)
            dec_g = jnp.concatenate(dec_parts, axis=1)
            ht_s[g] = ht_g * dec_g + _dot_tn(bg, xw_g)
        return carry

    lax.fori_loop(0, n_chunks, chunk_body, 0, unroll=True)

    yg = yb_s[...] * _silu(z_s[...])
    yn = _rms(yg, ng_ref[...]).astype(BF16)
    o = _dot(yn, wout_ref[...])
    y_ref[...] = x + _rms(o, gpost_ref[...])

    @pl.when(t == nt - 1)
    def _():
        for g in range(N_GROUPS):
            hg = ht_s[g].T
            for r in range(HEADS_PER_GROUP):
                hout_ref[g * HEADS_PER_GROUP + r] = hg[r * HEAD_DIM:(r + 1) * HEAD_DIM, :]


def _const_spec(shape):
    nd = len(shape)
    return pl.BlockSpec(shape, lambda *_: (0,) * nd, pipeline_mode=pl.Buffered(1))


def _ssd_prompt_layer(x, gpre, win, wdt, cw, cb, dtb, alog, dexp, ng, wout, gpost, *, tile):
    bsz, seq, d = x.shape
    d_inner = wout.shape[0]
    conv_dim = cw.shape[1]
    gn = N_GROUPS * D_STATE
    heads = d_inner // HEAD_DIM
    kw = cw.shape[0]
    consts = (gpre, win, wdt, cw, cb, dtb, alog, dexp, ng, wout, gpost)
    return pl.pallas_call(
        _ssd_prompt_kernel,
        grid=(bsz, seq // tile),
        in_specs=[pl.BlockSpec((None, tile, d), lambda b, t: (b, t, 0))]
        + [_const_spec(c.shape) for c in consts],
        out_specs=[
            pl.BlockSpec((None, tile, d), lambda b, t: (b, t, 0)),
            pl.BlockSpec((None, heads, HEAD_DIM, D_STATE), lambda b, t: (b, 0, 0, 0)),
            pl.BlockSpec((None, kw - 1, conv_dim), lambda b, t: (b, 0, 0)),
        ],
        out_shape=[
            jax.ShapeDtypeStruct((bsz, seq, d), F32),
            jax.ShapeDtypeStruct((bsz, heads, HEAD_DIM, D_STATE), F32),
            jax.ShapeDtypeStruct((bsz, kw - 1, conv_dim), F32),
        ],
        scratch_shapes=[
            pltpu.VMEM((N_GROUPS, D_STATE, HEADS_PER_GROUP * HEAD_DIM), F32),
            pltpu.VMEM((8, conv_dim), F32),
            pltpu.VMEM((tile, d_inner), F32),
            pltpu.VMEM((tile, d_inner), F32),
            pltpu.VMEM((tile, gn), BF16),
            pltpu.VMEM((tile, gn), BF16),
            pltpu.VMEM((tile, d_inner), F32),
            pltpu.VMEM((tile, LANES), F32),
            pltpu.VMEM((tile, d_inner), F32),
        ],
        compiler_params=pltpu.CompilerParams(
            dimension_semantics=("arbitrary", "arbitrary"),
            vmem_limit_bytes=VMEM_LIMIT_BYTES),
        name="ssd_prompt_layer",
    )(x, *consts)


def _cfm_prompt_kernel(xa_ref, xb_ref, gpre_ref, win_ref, bin_ref,
                       cw_ref, cb_ref, lng_ref, lnb_ref, wout_ref, gpost_ref,
                       y_ref, cout_ref, up_s, ext_s, z_s, acc_s, taps_s, *, tiles_per_seq):
    s = pl.program_id(0)
    n_tiles = pl.num_programs(0) - 1
    tile = xa_ref.shape[0]
    n_cblk = win_ref.shape[0]
    cblk = win_ref.shape[2] // 3
    lanes_per_cblk = cblk // LANES
    kw = cw_ref.shape[1]
    nslab = tile // 8
    nh = kw - 1
    hb = CFM_HALO_SLABS * 8
    nxt = s % 2
    cur = 1 - nxt
    first_of_seq = (jnp.minimum(s, n_tiles - 1) % tiles_per_seq) == 0

    @pl.when(s == 0)
    def _():
        ext_s[...] = jnp.zeros_like(ext_s)
        z_s[...] = jnp.zeros_like(z_s)
        for lb in range(n_cblk * lanes_per_cblk):
            for k in range(kw):
                taps_s[lb, k * 8:(k + 1) * 8, :] = jnp.broadcast_to(cw_ref[lb, k:k + 1, :], (8, LANES))

    ri = lax.broadcasted_iota(jnp.int32, (tile, tile), 0)
    ci = lax.broadcasted_iota(jnp.int32, (tile, tile), 1)
    to_perm = (ci == (ri & 7) * nslab + (ri >> 3)).astype(BF16)
    to_nat = (ri == (ci & 7) * nslab + (ci >> 3)).astype(BF16)
    u = _rms(xa_ref[...], gpre_ref[...]).astype(BF16)
    up_s[0:tile, :] = _dot(to_perm, u).astype(BF16)
    sub = lax.broadcasted_iota(jnp.int32, (nh * 8, LANES), 0) & 7

    def block_body(c, carry):
        lbs = [c * lanes_per_cblk + e for e in range(lanes_per_cblk)]
        prev_tails = [ext_s[cur, lb, hb + (nslab - nh) * 8:hb + tile, :] for lb in lbs]

        def conv_unit(j0, lb):
            accs = [jnp.broadcast_to(cb_ref[lb], (8, LANES))] * CFM_SLAB_GROUP
            for m in range(j0, j0 + CFM_SLAB_GROUP + kw - 1):
                e_m = ext_s[cur, lb, pl.ds((CFM_HALO_SLABS - nh + m) * 8, 8), :]
                for jj in range(CFM_SLAB_GROUP):
                    k = m - j0 - jj
                    if 0 <= k < kw:
                        accs[jj] = accs[jj] + taps_s[lb, pl.ds(k * 8, 8), :] * e_m
            for jj in range(CFM_SLAB_GROUP):
                acc_s[lb, pl.ds((j0 + jj) * 8, 8), :] = accs[jj]
            return accs[-1]

        def gated_input(after):
            up_s[tile:tile + 16, 0:LANES] = jnp.concatenate([after, after], axis=0).astype(BF16)
            return up_s[pl.ds(pl.multiple_of(jnp.minimum(s, 0) * 16, 16), tile), :]

        units = [(j0, lb) for j0 in range(0, nslab, CFM_SLAB_GROUP) for lb in lbs]
        proj = _dot(gated_input(conv_unit(*units[0])), win_ref[c]) + bin_ref[c]
        part = [proj[:, i * LANES:(i + 1) * LANES] for i in range(6)]
        z_s[nxt, c] = jnp.concatenate([part[3], part[5]], axis=1)
        for unit in units[1:]:
            conv_unit(*unit)
        v = jnp.concatenate([part[1] * _sigmoid(part[0]), part[4] * _sigmoid(part[2])], axis=1)
        cout_ref[c] = v.reshape(nslab, 8, cblk)[nslab - nh:, 7, :]
        for e in range(lanes_per_cblk):
            lb = c * lanes_per_cblk + e
            v_e = v[:, e * LANES:(e + 1) * LANES]
            prev_tail = jnp.where(first_of_seq, 0.0, prev_tails[e])
            mixed = jnp.where(sub == 7, prev_tail, v_e[(nslab - nh) * 8:, :]).reshape(nh, 8, LANES)
            halo = jnp.concatenate([mixed[:, 7:8, :], mixed[:, 0:7, :]], axis=1)
            ext_s[nxt, lb, hb - nh * 8:hb, :] = halo.reshape(nh * 8, LANES)
            ext_s[nxt, lb, hb:hb + tile, :] = v_e
        return carry

    lax.fori_loop(0, n_cblk, block_body, 0)

    acc = jnp.concatenate([acc_s[lb] for lb in range(n_cblk * lanes_per_cblk)], axis=1)
    xc = acc - jnp.mean(acc, axis=-1, keepdims=True)
    rstd = lax.rsqrt(jnp.mean(xc * xc, axis=-1, keepdims=True) + EPS)
    o = jnp.zeros(y_ref.shape, F32)
    for c in range(n_cblk):
        cols = slice(c * cblk, (c + 1) * cblk)
        cact = _silu(xc[:, cols] * rstd * lng_ref[:, cols] + lnb_ref[:, cols])
        m_p = (cact * _silu(z_s[cur, c])).astype(BF16)
        m_c = _dot(to_nat, m_p).astype(BF16)
        o = o + _dot(m_c, wout_ref[cols, :])
    y_ref[...] = xb_ref[...] + _rms(o, gpost_ref[...])


def _cfm_prompt_layer(x, gpre, win, bin_, cw, cb, lng, lnb, wout, gpost, *, tile):
    bsz, seq, d = x.shape
    n_cblk = win.shape[0]
    cblk = win.shape[2] // 3
    width = n_cblk * cblk
    n_lblk, kw, _ = cw.shape
    tiles_per_seq = seq // tile
    n_tiles = bsz * tiles_per_seq
    assert kw - 1 <= CFM_HALO_SLABS <= tile // 8 and seq % tile == 0 and n_lblk * LANES == width
    assert cblk == 2 * LANES
    consts = (gpre, win, bin_, cw, cb, lng, lnb, wout, gpost)

    def tile_a(s):
        ta = jnp.minimum(s, n_tiles - 1)
        return ta // tiles_per_seq, ta % tiles_per_seq

    def tile_b(s):
        tb = jnp.maximum(s - 1, 0)
        return tb // tiles_per_seq, tb % tiles_per_seq

    y, cout = pl.pallas_call(
        functools.partial(_cfm_prompt_kernel, tiles_per_seq=tiles_per_seq),
        grid=(n_tiles + 1,),
        in_specs=[pl.BlockSpec((None, tile, d), lambda s: (*tile_a(s), 0)),
                  pl.BlockSpec((None, tile, d), lambda s: (*tile_b(s), 0))]
        + [_const_spec(c.shape) for c in consts],
        out_specs=[
            pl.BlockSpec((None, tile, d), lambda s: (*tile_b(s), 0)),
            pl.BlockSpec((None, n_cblk, kw - 1, cblk), lambda s: (tile_a(s)[0], 0, 0, 0)),
        ],
        out_shape=[
            jax.ShapeDtypeStruct((bsz, seq, d), F32),
            jax.ShapeDtypeStruct((bsz, n_cblk, kw - 1, cblk), F32),
        ],
        scratch_shapes=[
            pltpu.VMEM((tile + 16, d), BF16),
            pltpu.VMEM((2, n_lblk, CFM_HALO_SLABS * 8 + tile, LANES), F32),
            pltpu.VMEM((2, n_cblk, tile, cblk), F32),
            pltpu.VMEM((n_lblk, tile, LANES), F32),
            pltpu.VMEM((n_lblk, kw * 8, LANES), F32),
        ],
        compiler_params=pltpu.CompilerParams(
            dimension_semantics=("arbitrary",),
            vmem_limit_bytes=VMEM_LIMIT_BYTES),
        name="cfm_prompt_layer",
    )(x, x, *consts)
    return y, jnp.transpose(cout, (0, 2, 1, 3)).reshape(bsz, kw - 1, width)


def _single_step_call(kernel, out_shapes, *args, name):
    return pl.pallas_call(
        kernel,
        grid=(1,),
        in_specs=[_const_spec(a.shape) for a in args],
        out_specs=[_const_spec(s.shape) for s in out_shapes],
        out_shape=out_shapes,
        compiler_params=pltpu.CompilerParams(
            dimension_semantics=("arbitrary",),
            vmem_limit_bytes=VMEM_LIMIT_BYTES),
        name=name,
    )(*args)


def _ssd_sample_front_kernel(x_ref, gpre_ref, win_ref, wdt_ref, cst_ref, cw_ref, cb_ref,
                             dtb_ref, alog_ref,
                             z_ref, xs_ref, b_ref, c_ref, xdt_t_ref, da_ref, cst_out_ref):
    d_inner = xs_ref.shape[1]
    gn = N_GROUPS * D_STATE
    kw = cw_ref.shape[0]
    u = _rms(x_ref[...], gpre_ref[...]).astype(BF16)
    z_ref[...] = _dot(u, win_ref[:, 0:d_inner])
    xbc = _dot(u, win_ref[:, d_inner:d_inner + cw_ref.shape[1]])
    dt_raw = _dot(u, wdt_ref[...])

    acc = cb_ref[...] + cw_ref[kw - 1:kw, :] * xbc
    for k in range(kw - 1):
        acc = acc + cw_ref[k:k + 1, :] * cst_ref[k]
    for k in range(kw - 2):
        cst_out_ref[k] = cst_ref[k + 1]
    cst_out_ref[kw - 2] = xbc
    act = _silu(acc)
    xs = act[:, :d_inner]
    xs_ref[...] = xs
    b_ref[...] = act[:, d_inner:d_inner + gn]
    c_ref[...] = act[:, d_inner + gn:]

    dt = _softplus(dt_raw + dtb_ref[...])
    da_ref[...] = jnp.exp(dt * (-jnp.exp(alog_ref[...])))[:, :da_ref.shape[1]]
    hrow = lax.broadcasted_iota(jnp.int32, (LANES, d_inner), 0)
    ccol = lax.broadcasted_iota(jnp.int32, (LANES, d_inner), 1)
    expand = ((ccol >> 6) == hrow).astype(BF16)
    xdt_t_ref[...] = (xs * _dot_f32_lhs(dt, expand)).T.astype(BF16)


def _ssd_sample_state_kernel(da_ref, h_ref, xdt_t_ref, ball_ref, c_ref, hn_ref, yt_ref):
    step = pl.program_id(0)
    tok_block = h_ref.shape[0]
    ntok = ball_ref.shape[0]
    rpg = HEADS_PER_GROUP * HEAD_DIM
    tok_row = lax.broadcasted_iota(jnp.int32, (ntok, D_STATE), 0)
    tok_lane = lax.broadcasted_iota(jnp.int32, (rpg, ntok), 1)
    blk_row = lax.broadcasted_iota(jnp.int32, (tok_block, D_STATE), 0)

    @pl.when(step == 0)
    def _():
        yt_ref[...] = jnp.zeros_like(yt_ref)

    def token_body(j, carry):
        tok = step * tok_block + j
        for g in range(N_GROUPS):
            gl = slice(g * D_STATE, (g + 1) * D_STATE)
            gr = slice(g * rpg, (g + 1) * rpg)
            b_tok = jnp.where(tok_row == tok, ball_ref[:, gl], 0.0).astype(BF16)
            outer = _dot(xdt_t_ref[gr, :], b_tok)
            scaled = [h_ref[j, (g * HEADS_PER_GROUP + r) * HEAD_DIM:(g * HEADS_PER_GROUP + r + 1) * HEAD_DIM, :]
                      * da_ref[tok, g * HEADS_PER_GROUP + r] for r in range(HEADS_PER_GROUP)]
            hn_g = jnp.concatenate(scaled, axis=0) + outer
            hn_ref[j, gr, :] = hn_g
            c_tok = jnp.sum(jnp.where(blk_row == j, c_ref[:, gl], 0.0), axis=0, keepdims=True)
            ycol = jnp.sum(hn_g * c_tok, axis=1, keepdims=True)
            yt_ref[gr, :] = jnp.where(tok_lane == tok, ycol, yt_ref[gr, :])
        return carry

    lax.fori_loop(0, tok_block, token_body, 0)


def _ssd_sample_state(da, h0, xdt_t, b_all, c_all, *, tok_block):
    ntok, rows, n = h0.shape
    return pl.pallas_call(
        _ssd_sample_state_kernel,
        grid=(ntok // tok_block,),
        in_specs=[
            pl.BlockSpec(memory_space=pltpu.SMEM),
            pl.BlockSpec((tok_block, rows, n), lambda i: (i, 0, 0)),
            _const_spec(xdt_t.shape),
            _const_spec(b_all.shape),
            pl.BlockSpec((tok_block, c_all.shape[1]), lambda i: (i, 0)),
        ],
        out_specs=[
            pl.BlockSpec((tok_block, rows, n), lambda i: (i, 0, 0)),
            pl.BlockSpec((rows, ntok), lambda i: (0, 0)),
        ],
        out_shape=[
            jax.ShapeDtypeStruct(h0.shape, F32),
            jax.ShapeDtypeStruct((rows, ntok), F32),
        ],
        compiler_params=pltpu.CompilerParams(
            dimension_semantics=("arbitrary",),
            vmem_limit_bytes=VMEM_LIMIT_BYTES),
        name="ssd_sample_state",
    )(da, h0, xdt_t, b_all, c_all)


def _ssd_sample_tail_kernel(yt_ref, xs_ref, z_ref, x_ref, dexp_ref, ng_ref, wout_ref, gpost_ref, o_ref):
    y = yt_ref[...].T + dexp_ref[...] * xs_ref[...]
    yg = y * _silu(z_ref[...])
    yn = _rms(yg, ng_ref[...]).astype(BF16)
    o = _dot(yn, wout_ref[...])
    o_ref[...] = x_ref[...] + _rms(o, gpost_ref[...])


def _cfm_sample_front_kernel(x_ref, gpre_ref, w_ref, bv_ref, bg_ref, bz_ref,
                             v_ref, z_ref):
    u = _rms(x_ref[...], gpre_ref[...]).astype(BF16)
    width = v_ref.shape[1]
    val = _dot(u, w_ref[:, 0:width]) + bv_ref[...]
    gate = _dot(u, w_ref[:, width:2 * width]) + bg_ref[...]
    z_ref[...] = _dot(u, w_ref[:, 2 * width:]) + bz_ref[...]
    v_ref[...] = val * _sigmoid(gate)


def _cfm_sample_conv_kernel(cst_ref, v_ref, cw_ref, cb_ref, c_ref, cst_out_ref):
    kw = cw_ref.shape[0]
    v = v_ref[...]
    acc = cb_ref[...] + cw_ref[kw - 1:kw, :] * v
    for k in range(kw - 1):
        acc = acc + cw_ref[k:k + 1, :] * cst_ref[k]
    c_ref[...] = acc
    for k in range(kw - 2):
        cst_out_ref[k] = cst_ref[k + 1]
    cst_out_ref[kw - 2] = v


def _cfm_sample_conv(cst, v, cw, cb, *, tok_block):
    taps, ntok, width = cst.shape
    return pl.pallas_call(
        _cfm_sample_conv_kernel,
        grid=(ntok // tok_block,),
        in_specs=[
            pl.BlockSpec((taps, tok_block, width), lambda i: (0, i, 0)),
            pl.BlockSpec((tok_block, width), lambda i: (i, 0)),
            _const_spec(cw.shape),
            _const_spec(cb.shape),
        ],
        out_specs=[
            pl.BlockSpec((tok_block, width), lambda i: (i, 0)),
            pl.BlockSpec((taps, tok_block, width), lambda i: (0, i, 0)),
        ],
        out_shape=[
            jax.ShapeDtypeStruct((ntok, width), F32),
            jax.ShapeDtypeStruct(cst.shape, F32),
        ],
        compiler_params=pltpu.CompilerParams(
            dimension_semantics=("arbitrary",),
            vmem_limit_bytes=VMEM_LIMIT_BYTES),
        name="cfm_sample_conv",
    )(cst, v, cw, cb)


def _cfm_sample_tail_kernel(c_ref, z_ref, x_ref, lng_ref, lnb_ref, wout_ref, gpost_ref, o_ref):
    c = _silu(_layer_norm(c_ref[...], lng_ref[...], lnb_ref[...]))
    o = _dot((c * _silu(z_ref[...])).astype(BF16), wout_ref[...])
    o_ref[...] = x_ref[...] + _rms(o, gpost_ref[...])


def _row(v):
    return v.reshape(1, -1)


def _pad_lanes(a):
    pad = (-a.shape[-1]) % LANES
    return jnp.pad(a, [(0, 0)] * (a.ndim - 1) + [(0, pad)])


def kernel(x_prompt, x_sample, state_ssm, state_conv_ssm, state_conv_cfm, g_pre, g_post, ssm_w_in, ssm_conv_w, ssm_conv_b, ssm_dt_bias, ssm_A_log, ssm_D, ssm_norm_g, ssm_w_out, cfm_w_in, cfm_b_in, cfm_conv_w, cfm_conv_b, cfm_ln_g, cfm_ln_b, cfm_w_out):
    d_inner = ssm_w_out.shape[1]
    conv_dim = ssm_conv_w.shape[2]
    heads = ssm_D.shape[1]
    width = cfm_w_out.shape[1]
    ntok = x_sample.shape[0]
    d = x_sample.shape[2]

    w_in = ssm_w_in[0]
    win0 = w_in.astype(BF16)
    wdt = _pad_lanes(w_in[:, d_inner + conv_dim:]).astype(BF16)
    cw0, cb0 = ssm_conv_w[0], _row(ssm_conv_b[0])
    dtb = _pad_lanes(_row(ssm_dt_bias[0]))
    alog = _pad_lanes(_row(ssm_A_log[0]))
    dexp = _row(jnp.repeat(ssm_D[0], d_inner // heads))
    ng = _row(ssm_norm_g[0])
    wout0 = ssm_w_out[0].astype(BF16)
    gpre0, gpost0 = _row(g_pre[0]), _row(g_post[0])

    w1 = cfm_w_in[0]
    w1b = w1.astype(BF16)
    wv, wg, wz1 = (w1b[:, i * width:(i + 1) * width] for i in range(3))
    bv, bg, bz = (_row(cfm_b_in[0][i * width:(i + 1) * width]) for i in range(3))
    cw1, cb1 = cfm_conv_w[0], _row(cfm_conv_b[0])
    lng, lnb = _row(cfm_ln_g[0]), _row(cfm_ln_b[0])
    wout1 = cfm_w_out[0].astype(BF16)
    gpre1, gpost1 = _row(g_pre[1]), _row(g_post[1])

    xp1, ssm_p, cssm_p = _ssd_prompt_layer(
        x_prompt, gpre0, win0, wdt, cw0, cb0, dtb, alog, dexp, ng, wout0, gpost0, tile=256)
    def col_blocks(w):
        return jnp.transpose(w.reshape(w.shape[0], width // CFM_CBLK, CFM_CBLK), (1, 0, 2))

    def mixed_blocks(g, v, z):
        g, v, z = col_blocks(g), col_blocks(v), col_blocks(z)
        lo, hi = slice(0, LANES), slice(LANES, 2 * LANES)
        return jnp.concatenate([g[..., lo], v[..., lo], g[..., hi], z[..., lo], v[..., hi], z[..., hi]], axis=2)

    def lane_blocks(a):
        return jnp.transpose(a.reshape(a.shape[0], width // LANES, LANES), (1, 0, 2))

    xp2, ccfm_p = _cfm_prompt_layer(
        xp1, gpre1,
        mixed_blocks(wg, wv, wz1), mixed_blocks(bg, bv, bz),
        lane_blocks(cw1), lane_blocks(cb1), lng, lnb, wout1, gpost1, tile=256)

    xs0 = x_sample.reshape(ntok, d)
    cst0 = jnp.transpose(state_conv_ssm[0], (1, 0, 2))
    gn = N_GROUPS * D_STATE
    sds = jax.ShapeDtypeStruct
    z_s, xs_s, b_s, c_s, xdt_t, da, cst0_new = _single_step_call(
        _ssd_sample_front_kernel,
        [sds((ntok, d_inner), F32), sds((ntok, d_inner), F32), sds((ntok, gn), F32),
         sds((ntok, gn), F32), sds((d_inner, ntok), BF16), sds((ntok, heads), F32),
         sds(cst0.shape, F32)],
        xs0, gpre0, win0, wdt, cst0, cw0, cb0, dtb, alog,
        name="ssd_sample_front")
    h0 = state_ssm[0].reshape(ntok, heads * HEAD_DIM, D_STATE)
    h_new, y_t = _ssd_sample_state(da, h0, xdt_t, b_s, c_s, tok_block=8)
    (xs1,) = _single_step_call(
        _ssd_sample_tail_kernel, [sds((ntok, d), F32)],
        y_t, xs_s, z_s, xs0, dexp, ng, wout0, gpost0, name="ssd_sample_tail")
    v_s, zc_s = _single_step_call(
        _cfm_sample_front_kernel, [sds((ntok, width), F32), sds((ntok, width), F32)],
        xs1, gpre1, w1b, bv, bg, bz, name="cfm_sample_front")
    c_conv, cst1_new = _cfm_sample_conv(
        jnp.transpose(state_conv_cfm[0], (1, 0, 2)), v_s, cw1, cb1, tok_block=16)
    (xs2,) = _single_step_call(
        _cfm_sample_tail_kernel, [sds((ntok, d), F32)],
        c_conv, zc_s, xs1, lng, lnb, wout1, gpost1, name="cfm_sample_tail")

    return (xp2,
            xs2.reshape(x_sample.shape),
            ssm_p[None],
            cssm_p[None],
            ccfm_p[None],
            h_new.reshape(state_ssm.shape),
            jnp.transpose(cst0_new, (1, 0, 2))[None],
            jnp.transpose(cst1_new, (1, 0, 2))[None])
```
